```python
import math
import jax, jax.numpy as jnp
from jax import lax
import numpy as np


D_MODEL = 4096
BATCH = 2
SEQ = 4096
DEPTH = 2

CHUNK = 64
QBLK = 128
D_MIX = D_MODEL
DIFF_WIDTH = D_MIX // 2
DIFF_HEADS = 16
DIFF_DV = DIFF_WIDTH // DIFF_HEADS
DIFF_DK = DIFF_DV // 2
RWKV_WIDTH = D_MIX - DIFF_WIDTH
RWKV_HEAD = 64
RWKV_HEADS = RWKV_WIDTH // RWKV_HEAD
LORA_W = 64
LORA_A = 64
LORA_G = 128
RWKV_COLS = 3 * RWKV_WIDTH + LORA_W + LORA_A + LORA_G
IN_COLS = 3 * DIFF_WIDTH + RWKV_COLS
N_GROUPS = 4
EXPERTS_PER_GROUP = 8
N_EXPERTS = N_GROUPS * EXPERTS_PER_GROUP
TOP_K = 2
D_EXPERT = 768
EXPERT_BLK = 128
RMS_EPS = 1e-6
HEAD_NORM_EPS = 1e-5
GN_EPS = 64e-5
DECAY_SCALE = math.exp(-0.5)

kernel_name = 'hybrid_diffattn_rwkv7_hiermoe'


def _rmsnorm(x, g, eps=RMS_EPS):
    xf = x.astype(jnp.float32)
    y = xf * lax.rsqrt(jnp.mean(xf * xf, axis=-1, keepdims=True) + eps)
    return (y * g.astype(jnp.float32)).astype(x.dtype)


def _alibi_slopes(n_heads):
    return jnp.asarray(2.0 ** (-8.0 * np.arange(1, n_heads + 1) / n_heads), dtype=jnp.float32)


def _diff_attention(q, k, v, lam, head_norm, lam_init):
    b, h, _, seq, _ = q.shape
    dv = v.shape[-1]
    n_blk = seq // QBLK
    slopes = _alibi_slopes(h)
    key_pos = jnp.arange(seq)
    key_chunk = key_pos // CHUNK
    scale = DIFF_DK ** -0.5

    def one_block(i):
        start = i * QBLK
        qb = lax.dynamic_slice_in_dim(q, start, QBLK, axis=3)
        q_pos = start + jnp.arange(QBLK)
        s = jnp.einsum('bhmqd,bhmkd->bhmqk', qb, k).astype(jnp.float32) * scale
        dist = jnp.abs(q_pos[:, None] - key_pos[None, :]).astype(jnp.float32)
        s = s - (slopes[:, None, None] * dist)[None, :, None]
        allowed = key_chunk[None, :] <= (q_pos // CHUNK)[:, None]
        s = jnp.where(allowed, s, -jnp.inf)
        p = jax.nn.softmax(s, axis=-1)
        attn = p[:, :, 0] - lam * p[:, :, 1]
        return jnp.einsum('bhqk,bhkd->bqhd', attn.astype(v.dtype), v)

    out = lax.map(one_block, jnp.arange(n_blk))
    out = jnp.moveaxis(out, 0, 1).reshape(b, seq, h, dv)
    out = _rmsnorm(out, head_norm, HEAD_NORM_EPS) * (1.0 - lam_init)
    return out.reshape(b, seq, h * dv)


def _rwkv7_time_mix(z, mu, w0, w_up, a0, a_up, g_up, k_k, k_a, r_k, ln_w, ln_b):
    b, seq, _ = z.shape
    c, hh, n = RWKV_WIDTH, RWKV_HEADS, RWKV_HEAD
    f32 = jnp.float32
    z_prev = jnp.pad(z, ((0, 0), (1, 0), (0, 0)))[:, :seq]
    z = z + mu * (z_prev - z)
    r, k, v = z[..., :c], z[..., c:2 * c], z[..., 2 * c:3 * c]
    o = 3 * c
    d_w = z[..., o:o + LORA_W]
    d_a = z[..., o + LORA_W:o + LORA_W + LORA_A]
    d_g = z[..., o + LORA_W + LORA_A:]
    w = jnp.exp(-DECAY_SCALE * jax.nn.sigmoid((w0 + jnp.tanh(d_w) @ w_up).astype(f32)))
    a = jax.nn.sigmoid((a0 + d_a @ a_up).astype(f32))
    g = jax.nn.sigmoid(d_g) @ g_up

    def heads(t):
        return t.reshape(b, seq, hh, n).astype(f32)

    kk = heads(k * k_k)
    kk = kk / jnp.maximum(jnp.sqrt(jnp.sum(kk * kk, axis=-1, keepdims=True)), 1e-12)
    k_h = heads(k.astype(f32) * (1.0 + (a - 1.0) * k_a.astype(f32)))
    r_h, v_h, w_h, a_h = heads(r), heads(v), heads(w), heads(a)

    def step(state, inp):
        r_t, w_t, k_t, v_t, kk_t, a_t = inp
        sa = jnp.einsum('bhij,bhj->bhi', state, -kk_t)
        state = (state * w_t[:, :, None, :]
                 + sa[..., None] * (kk_t * a_t)[:, :, None, :]
                 + v_t[..., None] * k_t[:, :, None, :])
        return state, jnp.einsum('bhij,bhj->bhi', state, r_t)

    xs = tuple(jnp.swapaxes(t, 0, 1) for t in (r_h, w_h, k_h, v_h, kk, a_h))
    _, y = lax.scan(step, jnp.zeros((b, hh, n, n), f32), xs)
    y = jnp.swapaxes(y, 0, 1)
    mean = jnp.mean(y, axis=-1, keepdims=True)
    var = jnp.mean((y - mean) ** 2, axis=-1, keepdims=True)
    y = (y - mean) * lax.rsqrt(var + GN_EPS) * ln_w.astype(f32).reshape(hh, n) + ln_b.astype(f32).reshape(hh, n)
    y = y + jnp.sum(r_h * k_h * r_k.astype(f32), axis=-1, keepdims=True) * v_h
    return (y.reshape(b, seq, c) * g.astype(f32)).astype(z.dtype)


def _hier_moe(h, w_rg, b_rg, w_re, b_re, wg, wu, wd):
    b, seq, d = h.shape
    t = b * seq
    xt = h.reshape(t, d)
    f32 = jnp.float32
    g_logits = (xt @ w_rg + b_rg).astype(f32)
    p_group = jax.nn.softmax(g_logits, axis=-1)
    g_idx = jnp.argmax(g_logits, axis=-1)
    p_sel = jnp.take_along_axis(p_group, g_idx[:, None], axis=1)
    e_logits = (xt @ w_re + b_re).astype(f32).reshape(t, N_GROUPS, EXPERTS_PER_GROUP)
    e_sel = jnp.take_along_axis(e_logits, g_idx[:, None, None], axis=1)[:, 0]
    top_v, top_i = lax.top_k(e_sel, TOP_K)
    wts = jax.nn.softmax(top_v, axis=-1) * p_sel
    eid = g_idx[:, None] * EXPERTS_PER_GROUP + top_i
    n_assign = t * TOP_K
    e_flat = eid.reshape(n_assign)
    w_flat = wts.reshape(n_assign)
    tok_flat = jnp.repeat(jnp.arange(t, dtype=jnp.int32), TOP_K)
    order = jnp.argsort(e_flat)
    e_sorted, tok_sorted, w_sorted = e_flat[order], tok_flat[order], w_flat[order]
    counts = jnp.zeros((N_EXPERTS,), jnp.int32).at[e_flat].add(1)
    starts = jnp.cumsum(counts) - counts
    padded = (counts + EXPERT_BLK - 1) // EXPERT_BLK * EXPERT_BLK
    pends = jnp.cumsum(padded)
    pstarts = pends - padded
    dest = pstarts[e_sorted] + (jnp.arange(n_assign) - starts[e_sorted])
    n_blk = (n_assign + EXPERT_BLK - 1) // EXPERT_BLK + N_EXPERTS
    n_slots = n_blk * EXPERT_BLK
    slot_tok = jnp.full((n_slots,), t, jnp.int32).at[dest].set(tok_sorted)
    slot_w = jnp.zeros((n_slots,), f32).at[dest].set(w_sorted)
    blk_start = jnp.arange(n_blk) * EXPERT_BLK
    blk_e = jnp.minimum(jnp.sum(blk_start[:, None] >= pends[None, :], axis=1), N_EXPERTS - 1)
    x_pad = jnp.concatenate([xt, jnp.zeros((1, d), xt.dtype)], axis=0)

    def run_block(args):
        e, toks, w = args
        xb = x_pad[toks]
        hid = jax.nn.silu(xb @ wg[e]) * (xb @ wu[e])
        return ((hid @ wd[e]) * w[:, None]).astype(xt.dtype)

    outs = lax.map(run_block, (blk_e, slot_tok.reshape(n_blk, EXPERT_BLK), slot_w.reshape(n_blk, EXPERT_BLK)))
    y = jnp.zeros((t + 1, d), xt.dtype).at[slot_tok].add(outs.reshape(n_slots, d))[:t]
    return y.reshape(b, seq, d)


def setup_inputs(seed: int = 0) -> dict:
    key = jax.random.key(seed)
    ks = jax.random.split(key, 26)
    L = DEPTH

    def nrm(k, shape, scale):
        return jax.random.normal(k, shape, jnp.float32) * scale

    return {
        'x': nrm(ks[0], (BATCH, SEQ, D_MODEL), 1.0),
        'attn_norm': 1.0 + nrm(ks[1], (L, D_MODEL), 0.02),
        'w_in': nrm(ks[2], (L, D_MODEL, IN_COLS), D_MODEL ** -0.5),
        'diff_lambda': nrm(ks[3], (L, 4, DIFF_DK), 0.1),
        'diff_head_norm': 1.0 + nrm(ks[4], (L, DIFF_DV), 0.02),
        'rwkv_mu': jax.random.uniform(ks[5], (L, RWKV_COLS), jnp.float32),
        'rwkv_w0': nrm(ks[6], (L, RWKV_WIDTH), 1.0),
        'rwkv_w_up': nrm(ks[7], (L, LORA_W, RWKV_WIDTH), 0.1),
        'rwkv_a0': nrm(ks[8], (L, RWKV_WIDTH), 0.1),
        'rwkv_a_up': nrm(ks[9], (L, LORA_A, RWKV_WIDTH), LORA_A ** -0.5),
        'rwkv_g_up': nrm(ks[10], (L, LORA_G, RWKV_WIDTH), LORA_G ** -0.5),
        'rwkv_k_k': 0.85 + nrm(ks[11], (L, RWKV_WIDTH), 0.02),
        'rwkv_k_a': 1.0 + nrm(ks[12], (L, RWKV_WIDTH), 0.02),
        'rwkv_r_k': nrm(ks[13], (L, RWKV_HEADS, RWKV_HEAD), 0.1),
        'rwkv_ln_w': 1.0 + nrm(ks[14], (L, RWKV_WIDTH), 0.02),
        'rwkv_ln_b': nrm(ks[15], (L, RWKV_WIDTH), 0.01),
        'w_out': nrm(ks[16], (L, D_MIX, D_MODEL), D_MIX ** -0.5),
        'ffn_norm': 1.0 + nrm(ks[17], (L, D_MODEL), 0.02),
        'router_group': nrm(ks[18], (L, D_MODEL, N_GROUPS), D_MODEL ** -0.5),
        'router_group_bias': nrm(ks[19], (L, N_GROUPS), 0.01),
        'router_expert': nrm(ks[20], (L, D_MODEL, N_EXPERTS), D_MODEL ** -0.5),
        'router_expert_bias': nrm(ks[21], (L, N_EXPERTS), 0.01),
        'expert_w_gate': nrm(ks[22], (L, N_EXPERTS, D_MODEL, D_EXPERT), D_MODEL ** -0.5),
        'expert_w_up': nrm(ks[23], (L, N_EXPERTS, D_MODEL, D_EXPERT), D_MODEL ** -0.5),
        'expert_w_down': nrm(ks[24], (L, N_EXPERTS, D_EXPERT, D_MODEL), D_EXPERT ** -0.5),
        'final_norm': 1.0 + nrm(ks[25], (D_MODEL,), 0.02),
    }


def reference(x, attn_norm, w_in, diff_lambda, diff_head_norm, rwkv_mu, rwkv_w0, rwkv_w_up,
              rwkv_a0, rwkv_a_up, rwkv_g_up, rwkv_k_k, rwkv_k_a, rwkv_r_k, rwkv_ln_w, rwkv_ln_b,
              w_out, ffn_norm, router_group, router_group_bias, router_expert, router_expert_bias,
              expert_w_gate, expert_w_up, expert_w_down, final_norm):
    b, seq, _ = x.shape
    dw = DIFF_WIDTH
    for l in range(DEPTH):
        lam_init = 0.8 - 0.6 * math.exp(-0.3 * l)
        h = _rmsnorm(x, attn_norm[l])
        proj = h @ w_in[l]
        q = proj[..., :dw].reshape(b, seq, DIFF_HEADS, 2, DIFF_DK).transpose(0, 2, 3, 1, 4)
        k = proj[..., dw:2 * dw].reshape(b, seq, DIFF_HEADS, 2, DIFF_DK).transpose(0, 2, 3, 1, 4)
        v = proj[..., 2 * dw:3 * dw].reshape(b, seq, DIFF_HEADS, DIFF_DV).transpose(0, 2, 1, 3)
        lam_p = diff_lambda[l].astype(jnp.float32)
        lam = jnp.exp(jnp.sum(lam_p[0] * lam_p[1])) - jnp.exp(jnp.sum(lam_p[2] * lam_p[3])) + lam_init
        y_diff = _diff_attention(q, k, v, lam, diff_head_norm[l], lam_init)
        y_rwkv = _rwkv7_time_mix(proj[..., 3 * dw:], rwkv_mu[l], rwkv_w0[l], rwkv_w_up[l], rwkv_a0[l],
                                 rwkv_a_up[l], rwkv_g_up[l], rwkv_k_k[l], rwkv_k_a[l], rwkv_r_k[l],
                                 rwkv_ln_w[l], rwkv_ln_b[l])
        x = x + jnp.concatenate([y_diff, y_rwkv], axis=-1) @ w_out[l]
        x = x + _hier_moe(_rmsnorm(x, ffn_norm[l]), router_group[l], router_group_bias[l],
                          router_expert[l], router_expert_bias[l], expert_w_gate[l],
                          expert_w_up[l], expert_w_down[l])
    return _rmsnorm(x, final_norm)
```

```python
import functools
import math

import jax
import jax.numpy as jnp
from jax import lax
from jax.experimental import pallas as pl
from jax.experimental.pallas import tpu as pltpu

F32 = jnp.float32
BF16 = jnp.bfloat16

LANES = 128
VMEM_BUDGET_BYTES = 56 << 20

MASK_CHUNK = 64
RMS_EPS = 1e-6
HEAD_NORM_EPS = 1e-5
GN_EPS = 64e-5
DECAY_SCALE = math.exp(-0.5)
TOP_K = 2
LORA_W = 64
LORA_A = 64
LORA_G = 128

RWKV_CHUNK = 64
MOE_BLK = 256


def _tile(n, candidates):
    return next(c for c in candidates if n % c == 0)


def _params(semantics):
    return pltpu.CompilerParams(dimension_semantics=semantics, vmem_limit_bytes=VMEM_BUDGET_BYTES)


def _dot(a, b):
    return jnp.dot(a, b, preferred_element_type=F32)


def _dot_nt(a, b):
    return lax.dot_general(a, b, (((1,), (1,)), ((), ())), preferred_element_type=F32)


def _dot_tn(a, b):
    return lax.dot_general(a, b, (((0,), (0,)), ((), ())), preferred_element_type=F32)


def _split2(x):
    hi = x.astype(BF16)
    lo = (x - hi.astype(F32)).astype(BF16)
    return hi, lo


def _split3(x):
    hi = x.astype(BF16)
    r1 = x - hi.astype(F32)
    mid = r1.astype(BF16)
    lo = (r1 - mid.astype(F32)).astype(BF16)
    return hi, mid, lo


def _dot_x3(a, b):
    ah, al = _split2(a)
    bh, bl = _split2(b)
    return _dot(ah, bh) + _dot(ah, bl) + _dot(al, bh)


def _iota(shape, dim):
    return lax.broadcasted_iota(jnp.int32, shape, dim)


def _sigmoid(x):
    return 1.0 / (1.0 + jnp.exp(-x))


def _rmsnorm_kernel(x_ref, g_ref, o_ref, *, eps):
    x = x_ref[...]
    ms = jnp.mean(x * x, axis=-1, keepdims=True)
    o_ref[...] = (x * lax.rsqrt(ms + eps) * g_ref[...]).astype(o_ref.dtype)


def _rmsnorm(x, g, out_dtype, tm=256):
    t, d = x.shape
    return pl.pallas_call(
        functools.partial(_rmsnorm_kernel, eps=RMS_EPS),
        out_shape=jax.ShapeDtypeStruct((t, d), out_dtype),
        grid=(t // tm,),
        in_specs=[pl.BlockSpec((tm, d), lambda i: (i, 0)), pl.BlockSpec((1, d), lambda i: (0, 0))],
        out_specs=pl.BlockSpec((tm, d), lambda i: (i, 0)),
        compiler_params=_params(("parallel",)),
        name="rmsnorm",
    )(x, g.reshape(1, d))


def _mm_kernel(*refs, n_a, has_res):
    a_refs, w_refs = refs[:n_a], refs[n_a:2 * n_a]
    o_ref = refs[-1]
    acc = None
    for a_ref, w_ref in zip(a_refs, w_refs):
        d = _dot(a_ref[...], w_ref[...].astype(BF16))
        acc = d if acc is None else acc + d
    if has_res:
        acc = acc + refs[2 * n_a][...]
    o_ref[...] = acc.astype(o_ref.dtype)


def _matmul(a_list, w, col_off, n_out, out_dtype, tm, tn, res=None, name="matmul"):
    m, ka = a_list[0].shape
    n_a = len(a_list)
    assert w.shape[0] == ka * n_a and col_off % tn == 0 and n_out % tn == 0 and m % tm == 0
    in_specs = [pl.BlockSpec((tm, ka), lambda i, j: (i, 0)) for _ in a_list]
    in_specs += [pl.BlockSpec((ka, tn), functools.partial(lambda i, j, r, c: (r, j + c), r=r, c=col_off // tn))
                 for r in range(n_a)]
    args = list(a_list) + [w] * n_a
    if res is not None:
        in_specs.append(pl.BlockSpec((tm, tn), lambda i, j: (i, j)))
        args.append(res)
    return pl.pallas_call(
        functools.partial(_mm_kernel, n_a=n_a, has_res=res is not None),
        out_shape=jax.ShapeDtypeStruct((m, n_out), out_dtype),
        grid=(m // tm, n_out // tn),
        in_specs=in_specs,
        out_specs=pl.BlockSpec((tm, tn), lambda i, j: (i, j)),
        compiler_params=_params(("parallel", "arbitrary")),
        name=name,
    )(*args)


def _attn_kernel(slopes_ref, lam_ref, q_ref, k_ref, v_ref, hn_ref, o_ref, *, tq, tk, dk, lam_init):
    h = pl.program_id(1)
    qi = pl.program_id(2)
    slope = slopes_ref[h]
    q0 = qi * tq
    dv = q_ref.shape[1]

    lam_p = lam_ref[...]
    lam = (jnp.exp(jnp.sum(lam_p[0:1] * lam_p[1:2], keepdims=True))
           - jnp.exp(jnp.sum(lam_p[2:3] * lam_p[3:4], keepdims=True)) + lam_init)

    q = q_ref[...] * (dk ** -0.5)
    lane = _iota((tq, dv), 1)
    zero = jnp.zeros_like(q)
    q2 = jnp.concatenate([jnp.where(lane < dk, q, zero), jnp.where(lane >= dk, q, zero)], axis=0)

    def step(k0, masked, carry):
        m, l, acc = carry
        k = k_ref[pl.ds(k0, tk), :]
        v = v_ref[pl.ds(k0, tk), :]
        s = _dot_nt(q2, k)
        kpos = k0 + _iota((1, tk), 1)
        if masked:
            r = _iota((2 * tq, 1), 0)
            r = jnp.where(r >= tq, r - tq, r)
            qpos = q0 + r
            bias = slope * (r - jnp.abs(qpos - kpos)).astype(F32)
            shift = MASK_CHUNK.bit_length() - 1
            allowed = jnp.right_shift(kpos, shift) <= jnp.right_shift(qpos, shift)
            s = jnp.where(allowed, s + bias, -jnp.inf)
        else:
            s = s + slope * (kpos - q0).astype(F32)
        m_new = jnp.maximum(m, jnp.max(s, axis=-1, keepdims=True))
        alpha = jnp.exp(m - m_new)
        p = jnp.exp(s - m_new)
        l = alpha * l + jnp.sum(p, axis=-1, keepdims=True)
        acc = alpha * acc + _dot(p.astype(BF16), v)
        return m_new, l, acc

    carry = (jnp.full((2 * tq, 1), -jnp.inf, F32), jnp.zeros((2 * tq, 1), F32), jnp.zeros((2 * tq, dv), F32))
    carry = lax.fori_loop(0, qi * (tq // tk), lambda i, c: step(pl.multiple_of(i * tk, tk), False, c), carry)
    for d in range(tq // tk):
        carry = step(pl.multiple_of(q0 + d * tk, tk), True, carry)
    _, l, acc = carry
    o = acc / l
    o = o[:tq] - lam * o[tq:]
    ms = jnp.mean(o * o, axis=-1, keepdims=True)
    o = o * lax.rsqrt(ms + HEAD_NORM_EPS) * hn_ref[...]
    o_ref[...] = (o * (1.0 - lam_init)).astype(o_ref.dtype)


def _diff_attention(proj, lam_p, head_norm, slopes, b, s, n_heads, lam_init, tq=256, tk=256):
    t = proj.shape[0]
    dv = head_norm.shape[-1]
    dk = lam_p.shape[-1]
    nq = s // tq
    return pl.pallas_call(
        functools.partial(_attn_kernel, tq=tq, tk=tk, dk=dk, lam_init=lam_init),
        out_shape=jax.ShapeDtypeStruct((t, n_heads * dv), BF16),
        grid=(b, n_heads, nq),
        in_specs=[
            pl.BlockSpec(memory_space=pltpu.SMEM),
            pl.BlockSpec(lam_p.shape, lambda bi, h, qi: (0, 0)),
            pl.BlockSpec((tq, dv), lambda bi, h, qi: (bi * nq + qi, h)),
            pl.BlockSpec((s, dv), lambda bi, h, qi: (bi, n_heads + h)),
            pl.BlockSpec((s, dv), lambda bi, h, qi: (bi, 2 * n_heads + h)),
            pl.BlockSpec((1, dv), lambda bi, h, qi: (0, 0)),
        ],
        out_specs=pl.BlockSpec((tq, dv), lambda bi, h, qi: (bi * nq + qi, h)),
        compiler_params=_params(("parallel", "parallel", "arbitrary")),
        name="diff_attention",
    )(slopes, lam_p, proj, proj, proj, head_norm.reshape(1, dv))


def _rwkv_chunk_kernel(zr_ref, zk_ref, zv_ref, zl_ref, hr_ref, hk_ref, hv_ref, hl_ref,
                       mur_ref, muk_ref, muv_ref, mul_ref, w0_ref, a0_ref, kk_ref, ka_ref, rk_ref,
                       wup_ref, aup_ref, gup_ref,
                       m_ref, g_ref, rp_ref, y0_ref, bonus_ref, gate_ref,
                       sr, sk, sv, sl, *, tc, n_head):
    c = RWKV_CHUNK
    n_chunks = tc // c
    first = pl.program_id(2) == 0
    w2 = 2 * n_head
    pt = 2 * c

    def shifted(z_ref, h_ref, mu_ref, out_ref):
        z = z_ref[...]
        prev_row = jnp.where(first, 0.0, h_ref[7:8, :])
        zp = pltpu.roll(z, 1, axis=0)
        zp = jnp.where(_iota(z.shape, 0) == 0, prev_row, zp)
        out_ref[...] = z + mu_ref[...] * (zp - z)

    shifted(zr_ref, hr_ref, mur_ref, sr)
    shifted(zk_ref, hk_ref, muk_ref, sk)
    shifted(zv_ref, hv_ref, muv_ref, sv)
    shifted(zl_ref, hl_ref, mul_ref, sl)

    lane = _iota((c, w2), 1)
    m0 = lane < n_head
    row_p, col_p = _iota((pt, pt), 0), _iota((pt, pt), 1)
    same_head = (row_p // c) == (col_p // c)
    strict = same_head & ((row_p % c) > (col_p % c))
    incl = same_head & ((row_p % c) >= (col_p % c))
    eye_p = (row_p == col_p).astype(F32)
    row_l, col_l = _iota((w2, w2), 0), _iota((w2, w2), 1)
    head_blk = (row_l // n_head) == (col_l // n_head)
    seg_ones = head_blk.astype(BF16)
    tri = (_iota((c, c), 0) >= _iota((c, c), 1)).astype(BF16)
    w_cat = jnp.concatenate([wup_ref[...], aup_ref[...]], axis=0)
    g_up = gup_ref[...].astype(BF16)

    def seg_sum(x):
        hi, lo = _split2(x)
        return _dot(hi, seg_ones) + _dot(lo, seg_ones)

    def stack(x):
        z = jnp.zeros_like(x)
        return jnp.concatenate([jnp.where(m0, x, z), jnp.where(m0, z, x)], axis=0)

    def unstack(x):
        return x[:c] + x[c:]

    def chunk(ci, _):
        rows = pl.ds(pl.multiple_of(ci * c, c), c)
        r, k, v = sr[rows, :], sk[rows, :], sv[rows, :]
        lora = sl[rows, :]
        d_wa = lora[:, :LORA_W + LORA_A]
        zero = jnp.zeros_like(d_wa)
        logit_w = w0_ref[...] + _dot_x3(jnp.where(m0, jnp.tanh(d_wa), zero), w_cat)
        logit_a = a0_ref[...] + _dot_x3(jnp.where(m0, zero, d_wa), w_cat)
        logw = -DECAY_SCALE * _sigmoid(logit_w)
        a = _sigmoid(logit_a)
        gate = _dot(_sigmoid(lora[:, LORA_W + LORA_A:]).astype(BF16), g_up)
        kk = k * kk_ref[...]
        kap = kk / jnp.maximum(jnp.sqrt(seg_sum(kk * kk)), 1e-12)
        kh = k * (1.0 + (a - 1.0) * ka_ref[...])
        bvec = kap * a
        bonus = seg_sum(r * kh * rk_ref[...]) * v

        lw_h, lw_m, lw_l = _split3(logw)
        cum = _dot(tri, lw_h) + _dot(tri, lw_m) + _dot(tri, lw_l)
        cum_end = cum[c - 1:c, :]
        dec_in = jnp.exp(cum - logw)
        dec = jnp.exp(cum)
        inv_dec = jnp.exp(-cum)
        dec_out = jnp.exp(cum_end - cum)
        kap_t = stack(kap * dec_in)
        r_t = stack(r * dec)
        v_s = stack(v).astype(BF16)

        lhs = jnp.concatenate([kap_t, r_t], axis=0).astype(BF16)
        rhs = jnp.concatenate([stack(bvec * inv_dec), stack(kh * inv_dec)], axis=0).astype(BF16)
        aa = _dot_nt(lhs, rhs)
        a_kb = jnp.where(strict, aa[:pt, :pt], 0.0)
        a_kk = jnp.where(strict, aa[:pt, pt:], 0.0)
        a_rb = jnp.where(incl, aa[pt:, :pt], 0.0).astype(BF16)
        a_rk = jnp.where(incl, aa[pt:, pt:], 0.0).astype(BF16)

        x = a_kb
        tinv = eye_p - x
        for _ in range(int(math.log2(c)) - 1):
            xb = x.astype(BF16)
            x = _dot(xb, xb)
            tinv = _dot(tinv.astype(BF16), (eye_p + x).astype(BF16))

        akv = _dot(a_kk.astype(BF16), v_s)
        pq = _dot(tinv.astype(BF16), jnp.concatenate([kap_t, akv], axis=1).astype(BF16))
        pq_b = pq.astype(BF16)
        arb_pq = _dot(a_rb, pq_b)
        rp = r_t - arb_pq[:, :w2]
        y0 = _dot(a_rk, v_s) - arb_pq[:, w2:]
        pq_s = unstack(pq).astype(BF16)
        b_out = (bvec * dec_out).astype(BF16)
        k_out = (kh * dec_out).astype(BF16)
        bt_pq = _dot_tn(b_out, pq_s)
        m_mat = eye_p[:w2, :w2] * jnp.exp(cum_end) - jnp.where(head_blk, bt_pq[:, :w2], 0.0)
        g_mat = jnp.where(head_blk, _dot_tn(k_out, v.astype(BF16)) - bt_pq[:, w2:], 0.0)

        m_ref[ci] = m_mat
        g_ref[ci] = g_mat
        rp_ref[rows, :] = unstack(rp)
        y0_ref[rows, :] = unstack(y0)
        bonus_ref[rows, :] = bonus
        gate_ref[rows, :] = gate
        return 0

    lax.fori_loop(0, n_chunks, chunk, 0)


def _rwkv_state_kernel(m_ref, g_ref, rp_ref, y0_ref, bonus_ref, gate_ref, lnw_ref, lnb_ref, o_ref, st_ref,
                       *, n_pairs_blk, n_head):
    @pl.when(pl.program_id(2) == 0)
    def _():
        st_ref[...] = jnp.zeros_like(st_ref)

    w2 = 2 * n_head
    head_blk = ((_iota((w2, w2), 0) // n_head) == (_iota((w2, w2), 1) // n_head)).astype(BF16)

    def seg_mean(x):
        hi, lo = _split2(x)
        return (_dot(hi, head_blk) + _dot(lo, head_blk)) * (1.0 / n_head)

    for p in range(n_pairs_blk):
        cols = slice(p * w2, (p + 1) * w2)
        st = st_ref[p]
        st_hi, st_lo = _split2(st)
        rp = rp_ref[:, cols].astype(BF16)
        y = _dot(rp, st_hi) + _dot(rp, st_lo) + y0_ref[:, cols]
        m = m_ref[p, 0].astype(BF16)
        st_ref[p] = _dot(m, st_hi) + _dot(m, st_lo) + g_ref[p, 0]
        mean = seg_mean(y)
        yc = y - mean
        var = seg_mean(yc * yc)
        yn = yc * lax.rsqrt(var + GN_EPS) * lnw_ref[:, cols] + lnb_ref[:, cols]
        o_ref[:, cols] = ((yn + bonus_ref[:, cols]) * gate_ref[:, cols]).astype(o_ref.dtype)


def _rwkv7(z, mu, w0, w_up, a0, a_up, g_up, k_k, k_a, r_k, ln_w, ln_b, b, s, tc=512, pairs_blk=4):
    t = z.shape[0]
    c = w0.shape[-1]
    n_head = r_k.shape[-1]
    w2 = 2 * n_head
    assert w2 == LANES and (3 * c) % (2 * LANES) == 0 and LORA_W + LORA_A == LANES
    n_pairs = c // w2
    tc = min(tc, s)
    pairs_blk = min(pairs_blk, n_pairs)
    n_tb = s // tc
    ch = RWKV_CHUNK
    n_chunks = s // ch
    cpb = tc // ch
    lw = LORA_W + LORA_A + LORA_G
    cb = c // w2
    lb = 3 * c // lw
    mu2 = mu.reshape(1, -1)
    row = lambda v_: v_.reshape(1, c)

    def zspec(col_blk_fn, width):
        return pl.BlockSpec((tc, width), lambda bi, p, ti: (bi * n_tb + ti, col_blk_fn(p)))

    def hspec(col_blk_fn, width):
        return pl.BlockSpec((8, width),
                            lambda bi, p, ti: (jnp.maximum((bi * n_tb + ti) * (tc // 8) - 1, 0), col_blk_fn(p)))

    def pspec(col_blk_fn, width, rows=1):
        return pl.BlockSpec((rows, width), lambda bi, p, ti: (0, col_blk_fn(p)))

    sec = [lambda p: p, lambda p: cb + p, lambda p: 2 * cb + p]
    in_specs = ([zspec(f, w2) for f in sec] + [zspec(lambda p: lb, lw)]
                + [hspec(f, w2) for f in sec] + [hspec(lambda p: lb, lw)]
                + [pspec(f, w2) for f in sec] + [pspec(lambda p: lb, lw)]
                + [pspec(sec[0], w2) for _ in range(5)]
                + [pspec(sec[0], w2, LORA_W), pspec(sec[0], w2, LORA_A), pspec(sec[0], w2, LORA_G)])
    mat_shape = jax.ShapeDtypeStruct((b * n_pairs, n_chunks, w2, w2), F32)
    slab_shape = jax.ShapeDtypeStruct((t, c), F32)
    mat_spec = pl.BlockSpec((None, cpb, w2, w2), lambda bi, p, ti: (bi * n_pairs + p, ti, 0, 0))
    slab_spec = pl.BlockSpec((tc, w2), lambda bi, p, ti: (bi * n_tb + ti, p))
    m_mat, g_mat, rp, y0, bonus, gate = pl.pallas_call(
        functools.partial(_rwkv_chunk_kernel, tc=tc, n_head=n_head),
        out_shape=[mat_shape, mat_shape, slab_shape, slab_shape, slab_shape, slab_shape],
        grid=(b, n_pairs, n_tb),
        in_specs=in_specs,
        out_specs=[mat_spec, mat_spec, slab_spec, slab_spec, slab_spec, slab_spec],
        scratch_shapes=[pltpu.VMEM((tc, w2), F32)] * 3 + [pltpu.VMEM((tc, lw), F32)],
        compiler_params=_params(("parallel", "parallel", "arbitrary")),
        name="rwkv_chunk",
    )(z, z, z, z, z, z, z, z, mu2, mu2, mu2, mu2, row(w0), row(a0), row(k_k), row(k_a), row(r_k),
      w_up, a_up, g_up)

    n_pb = n_pairs // pairs_blk
    wblk = pairs_blk * w2
    mat_in = pl.BlockSpec((pairs_blk, 1, w2, w2), lambda bi, pb, ci: (bi * n_pb + pb, ci, 0, 0))
    slab_in = pl.BlockSpec((ch, wblk), lambda bi, pb, ci: (bi * n_chunks + ci, pb))
    par_in = pl.BlockSpec((1, wblk), lambda bi, pb, ci: (0, pb))
    return pl.pallas_call(
        functools.partial(_rwkv_state_kernel, n_pairs_blk=pairs_blk, n_head=n_head),
        out_shape=jax.ShapeDtypeStruct((t, c), BF16),
        grid=(b, n_pb, n_chunks),
        in_specs=[mat_in, mat_in, slab_in, slab_in, slab_in, slab_in, par_in, par_in],
        out_specs=slab_in,
        scratch_shapes=[pltpu.VMEM((pairs_blk, w2, w2), F32)],
        compiler_params=_params(("parallel", "parallel", "arbitrary")),
        name="rwkv_state",
    )(m_mat, g_mat, rp, y0, bonus, gate, row(ln_w), row(ln_b))


def _router_kernel(x_ref, g_ref, w_ref, b_ref, h_ref, logit_ref, *, eps):
    x = x_ref[...]
    ms = jnp.mean(x * x, axis=-1, keepdims=True)
    h = x * lax.rsqrt(ms + eps) * g_ref[...]
    h_ref[...] = h
    logit_ref[...] = _dot_x3(h, w_ref[...]) + b_ref[...]


def _router(x, g, w_router, b_router, tm=256):
    t, d = x.shape
    n = w_router.shape[1]
    return pl.pallas_call(
        functools.partial(_router_kernel, eps=RMS_EPS),
        out_shape=[jax.ShapeDtypeStruct((t, d), F32), jax.ShapeDtypeStruct((t, n), F32)],
        grid=(t // tm,),
        in_specs=[pl.BlockSpec((tm, d), lambda i: (i, 0)), pl.BlockSpec((1, d), lambda i: (0, 0)),
                  pl.BlockSpec((d, n), lambda i: (0, 0)), pl.BlockSpec((1, n), lambda i: (0, 0))],
        out_specs=[pl.BlockSpec((tm, d), lambda i: (i, 0)), pl.BlockSpec((tm, n), lambda i: (i, 0))],
        compiler_params=_params(("parallel",)),
        name="moe_router",
    )(x, g.reshape(1, d), w_router, b_router)


def _gather_rows(idx_ref, base, src_hbm, dst_vmem, sem, n_rows):
    def issue(r, _):
        pltpu.make_async_copy(src_hbm.at[pl.ds(idx_ref[base + r], 1), :], dst_vmem.at[pl.ds(r, 1), :], sem).start()
        return 0
    lax.fori_loop(0, n_rows, issue, 0, unroll=8)


def _wait_rows(src_hbm, dst_vmem, sem, n_rows):
    pltpu.make_async_copy(src_hbm.at[pl.ds(0, n_rows), :], dst_vmem, sem).wait()


def _dispatch_kernel(tok_ref, h_hbm, o_ref, buf, sem, *, blk):
    base = pl.program_id(0) * blk
    _gather_rows(tok_ref, base, h_hbm, buf, sem, blk)
    _wait_rows(h_hbm, buf, sem, blk)
    o_ref[...] = buf[...].astype(o_ref.dtype)


def _dispatch(slot_tok, h, blk):
    n_slots = slot_tok.shape[0]
    d = h.shape[1]
    return pl.pallas_call(
        functools.partial(_dispatch_kernel, blk=blk),
        out_shape=jax.ShapeDtypeStruct((n_slots, d), BF16),
        grid_spec=pltpu.PrefetchScalarGridSpec(
            num_scalar_prefetch=1,
            grid=(n_slots // blk,),
            in_specs=[pl.BlockSpec(memory_space=pl.ANY)],
            out_specs=pl.BlockSpec((blk, d), lambda i, tok: (i, 0)),
            scratch_shapes=[pltpu.VMEM((blk, d), F32), pltpu.SemaphoreType.DMA(())],
        ),
        compiler_params=_params(("arbitrary",)),
        name="moe_dispatch",
    )(slot_tok, h)


def _expert_up_kernel(blk_e_ref, x_ref, wg_ref, wu_ref, o_ref):
    x = x_ref[...]
    gate = _dot(x, wg_ref[...].astype(BF16))
    up = _dot(x, wu_ref[...].astype(BF16))
    o_ref[...] = (gate * _sigmoid(gate) * up).astype(o_ref.dtype)


def _expert_down_kernel(blk_e_ref, h_ref, wd_ref, o_ref):
    o_ref[...] = _dot(h_ref[...], wd_ref[...].astype(BF16))


def _experts(blk_e, xs, wg, wu, wd, blk, tn=256):
    n_slots, d = xs.shape
    de = wg.shape[-1]
    n_blk = n_slots // blk
    tn = min(tn, de)
    hid = pl.pallas_call(
        _expert_up_kernel,
        out_shape=jax.ShapeDtypeStruct((n_slots, de), BF16),
        grid_spec=pltpu.PrefetchScalarGridSpec(
            num_scalar_prefetch=1,
            grid=(de // tn, n_blk),
            in_specs=[pl.BlockSpec((blk, d), lambda j, i, be: (i, 0)),
                      pl.BlockSpec((None, d, tn), lambda j, i, be: (be[i], 0, j)),
                      pl.BlockSpec((None, d, tn), lambda j, i, be: (be[i], 0, j))],
            out_specs=pl.BlockSpec((blk, tn), lambda j, i, be: (i, j)),
        ),
        compiler_params=_params(("parallel", "arbitrary")),
        name="moe_expert_up",
    )(blk_e, xs, wg, wu)
    return pl.pallas_call(
        _expert_down_kernel,
        out_shape=jax.ShapeDtypeStruct((n_slots, d), F32),
        grid_spec=pltpu.PrefetchScalarGridSpec(
            num_scalar_prefetch=1,
            grid=(n_blk,),
            in_specs=[pl.BlockSpec((blk, de), lambda i, be: (i, 0)),
                      pl.BlockSpec((None, de, d), lambda i, be: (be[i], 0, 0))],
            out_specs=pl.BlockSpec((blk, d), lambda i, be: (i, 0)),
        ),
        compiler_params=_params(("arbitrary",)),
        name="moe_expert_down",
    )(blk_e, hid, wd)


def _combine_kernel(pos0_ref, pos1_ref, x_ref, w_ref, y_hbm, o_ref, buf0, buf1, sem0, sem1, *, tb):
    base = pl.program_id(0) * tb
    _gather_rows(pos0_ref, base, y_hbm, buf0, sem0, tb)
    _gather_rows(pos1_ref, base, y_hbm, buf1, sem1, tb)
    w = w_ref[...]
    _wait_rows(y_hbm, buf0, sem0, tb)
    _wait_rows(y_hbm, buf1, sem1, tb)
    o_ref[...] = x_ref[...] + (buf0[...] * w[:, 0:1] + buf1[...] * w[:, 1:2])


def _combine(pos0, pos1, x, wts, y, tb=256):
    t, d = x.shape
    return pl.pallas_call(
        functools.partial(_combine_kernel, tb=tb),
        out_shape=jax.ShapeDtypeStruct((t, d), F32),
        grid_spec=pltpu.PrefetchScalarGridSpec(
            num_scalar_prefetch=2,
            grid=(t // tb,),
            in_specs=[pl.BlockSpec((tb, d), lambda i, p0, p1: (i, 0)),
                      pl.BlockSpec((tb, TOP_K), lambda i, p0, p1: (i, 0)),
                      pl.BlockSpec(memory_space=pl.ANY)],
            out_specs=pl.BlockSpec((tb, d), lambda i, p0, p1: (i, 0)),
            scratch_shapes=[pltpu.VMEM((tb, d), F32), pltpu.VMEM((tb, d), F32),
                            pltpu.SemaphoreType.DMA(()), pltpu.SemaphoreType.DMA(())],
        ),
        compiler_params=_params(("arbitrary",)),
        name="moe_combine",
    )(pos0, pos1, x, wts, y)


def _route(logits, n_groups, n_experts, blk):
    t = logits.shape[0]
    epg = n_experts // n_groups
    g_logits = logits[:, :n_groups]
    p_group = jax.nn.softmax(g_logits, axis=-1)
    g_idx = jnp.argmax(g_logits, axis=-1)
    p_sel = jnp.take_along_axis(p_group, g_idx[:, None], axis=1)
    e_logits = logits[:, n_groups:n_groups + n_experts].reshape(t, n_groups, epg)
    e_sel = jnp.take_along_axis(e_logits, g_idx[:, None, None], axis=1)[:, 0]
    top_v, top_i = lax.top_k(e_sel, TOP_K)
    wts = jax.nn.softmax(top_v, axis=-1) * p_sel
    eid = (g_idx[:, None] * epg + top_i).astype(jnp.int32)
    e_flat = eid.reshape(-1)
    n_assign = e_flat.shape[0]
    onehot = (e_flat[:, None] == jnp.arange(n_experts, dtype=jnp.int32)[None, :]).astype(jnp.int32)
    csum = jnp.cumsum(onehot, axis=0)
    counts = csum[-1]
    rank = jnp.take_along_axis(csum, e_flat[:, None], axis=1)[:, 0] - 1
    padded = (counts + blk - 1) // blk * blk
    pends = jnp.cumsum(padded)
    pstarts = pends - padded
    pos = (pstarts[e_flat] + rank).astype(jnp.int32)
    n_blk = (n_assign + blk - 1) // blk + n_experts
    tok_flat = jnp.repeat(jnp.arange(t, dtype=jnp.int32), TOP_K)
    slot_tok = jnp.zeros((n_blk * blk,), jnp.int32).at[pos].set(tok_flat)
    blk_start = jnp.arange(n_blk, dtype=jnp.int32) * blk
    blk_e = jnp.minimum(jnp.sum(blk_start[:, None] >= pends[None, :], axis=1), n_experts - 1).astype(jnp.int32)
    pos2 = pos.reshape(t, TOP_K)
    return slot_tok, blk_e, pos2[:, 0], pos2[:, 1], wts.astype(F32)


def _hier_moe(x, ffn_norm, w_rg, b_rg, w_re, b_re, wg, wu, wd):
    n_groups, n_experts = w_rg.shape[1], w_re.shape[1]
    n_r = n_groups + n_experts
    n_pad = -n_r % LANES
    w_router = jnp.pad(jnp.concatenate([w_rg, w_re], axis=1), ((0, 0), (0, n_pad)))
    b_router = jnp.pad(jnp.concatenate([b_rg, b_re], axis=0), (0, n_pad)).reshape(1, -1)
    h, logits = _router(x, ffn_norm, w_router, b_router)
    slot_tok, blk_e, pos0, pos1, wts = _route(logits, n_groups, n_experts, MOE_BLK)
    xs = _dispatch(slot_tok, h, MOE_BLK)
    y = _experts(blk_e, xs, wg, wu, wd, MOE_BLK)
    return _combine(pos0, pos1, x, wts, y)


def kernel(x, attn_norm, w_in, diff_lambda, diff_head_norm, rwkv_mu, rwkv_w0, rwkv_w_up, rwkv_a0, rwkv_a_up, rwkv_g_up, rwkv_k_k, rwkv_k_a, rwkv_r_k, rwkv_ln_w, rwkv_ln_b, w_out, ffn_norm, router_group, router_group_bias, router_expert, router_expert_bias, expert_w_gate, expert_w_up, expert_w_down, final_norm):
    b, s, d = x.shape
    depth = w_in.shape[0]
    t = b * s
    dv = diff_head_norm.shape[-1]
    rw = rwkv_w0.shape[-1]
    dw = d - rw
    n_heads = dw // dv
    attn_cols = 3 * dw
    rwkv_cols = w_in.shape[-1] - attn_cols
    slopes = jnp.asarray([2.0 ** (-8.0 * (i + 1) / n_heads) for i in range(n_heads)], F32)
    tm = min(1024, t)

    xt = x.reshape(t, d)
    for l in range(depth):
        lam_init = 0.8 - 0.6 * math.exp(-0.3 * l)
        h = _rmsnorm(xt, attn_norm[l], BF16)
        proj_a = _matmul([h], w_in[l], 0, attn_cols, BF16, tm, _tile(attn_cols, (512, 256, 128)), name="in_proj_attn")
        proj_r = _matmul([h], w_in[l], attn_cols, rwkv_cols, F32, tm, 256, name="in_proj_rwkv")
        y_diff = _diff_attention(proj_a, diff_lambda[l], diff_head_norm[l], slopes, b, s, n_heads, lam_init)
        y_rwkv = _rwkv7(proj_r, rwkv_mu[l], rwkv_w0[l], rwkv_w_up[l], rwkv_a0[l], rwkv_a_up[l], rwkv_g_up[l],
                        rwkv_k_k[l], rwkv_k_a[l], rwkv_r_k[l], rwkv_ln_w[l], rwkv_ln_b[l], b, s)
        xt = _matmul([y_diff, y_rwkv], w_out[l], 0, d, F32, tm, _tile(d, (512, 256, 128)), res=xt, name="out_proj")
        xt = _hier_moe(xt, ffn_norm[l], router_group[l], router_group_bias[l], router_expert[l],
                       router_expert_bias[l], expert_w_gate[l], expert_w_up[l], expert_w_down[l])
    return _rmsnorm(xt, final_norm, F32).reshape(b, s, d)
```

```python
import functools
import math

import jax
import jax.numpy as jnp
from jax import lax
from jax.experimental import pallas as pl
from jax.experimental.pallas import tpu as pltpu

F32 = jnp.float32
BF16 = jnp.bfloat16

LANES = 128
VMEM_BUDGET_BYTES = 56 << 20

MASK_CHUNK = 64
RMS_EPS = 1e-6
HEAD_NORM_EPS = 1e-5
GN_EPS = 64e-5
DECAY_SCALE = math.exp(-0.5)
TOP_K = 2
LORA_W = 64
LORA_A = 64
LORA_G = 128

ATTN_BLOCK = 512
RWKV_CHUNK = 64
RWKV_GROUP = 8
MOE_BLK = 256


def _tile(n, candidates):
    return next(c for c in candidates if n % c == 0)


def _params(semantics):
    return pltpu.CompilerParams(dimension_semantics=semantics, vmem_limit_bytes=VMEM_BUDGET_BYTES)


def _dot(a, b):
    return jnp.dot(a, b, preferred_element_type=F32)


def _dot_nt(a, b):
    return lax.dot_general(a, b, (((1,), (1,)), ((), ())), preferred_element_type=F32)


def _dot_tn(a, b):
    return lax.dot_general(a, b, (((0,), (0,)), ((), ())), preferred_element_type=F32)


def _split2(x):
    hi = x.astype(BF16)
    lo = (x - hi.astype(F32)).astype(BF16)
    return hi, lo


def _split3(x):
    hi = x.astype(BF16)
    r1 = x - hi.astype(F32)
    mid = r1.astype(BF16)
    lo = (r1 - mid.astype(F32)).astype(BF16)
    return hi, mid, lo


def _dot_x3(a, b):
    ah, al = _split2(a)
    bh, bl = _split2(b)
    return _dot(ah, bh) + _dot(ah, bl) + _dot(al, bh)


def _iota(shape, dim):
    return lax.broadcasted_iota(jnp.int32, shape, dim)


def _sigmoid(x):
    return 1.0 / (1.0 + jnp.exp(-x))


def _rmsnorm_kernel(x_ref, g_ref, o_ref, *, eps):
    x = x_ref[...]
    ms = jnp.mean(x * x, axis=-1, keepdims=True)
    o_ref[...] = (x * lax.rsqrt(ms + eps) * g_ref[...]).astype(o_ref.dtype)


def _rmsnorm(x, g, out_dtype, tm=256):
    t, d = x.shape
    return pl.pallas_call(
        functools.partial(_rmsnorm_kernel, eps=RMS_EPS),
        out_shape=jax.ShapeDtypeStruct((t, d), out_dtype),
        grid=(t // tm,),
        in_specs=[pl.BlockSpec((tm, d), lambda i: (i, 0)), pl.BlockSpec((1, d), lambda i: (0, 0))],
        out_specs=pl.BlockSpec((tm, d), lambda i: (i, 0)),
        compiler_params=_params(("parallel",)),
        name="rmsnorm",
    )(x, g.reshape(1, d))


def _mm_kernel(*refs, n_a, has_res):
    a_refs, w_refs = refs[:n_a], refs[n_a:2 * n_a]
    o_ref = refs[-1]
    acc = None
    for a_ref, w_ref in zip(a_refs, w_refs):
        d = _dot(a_ref[...], w_ref[...].astype(BF16))
        acc = d if acc is None else acc + d
    if has_res:
        acc = acc + refs[2 * n_a][...]
    o_ref[...] = acc.astype(o_ref.dtype)


def _matmul(a_list, w, layer, col_off, n_out, out_dtype, tm, tn, res=None, name="matmul"):
    m, ka = a_list[0].shape
    n_a = len(a_list)
    assert w.shape[1] == ka * n_a and col_off % tn == 0 and n_out % tn == 0 and m % tm == 0
    in_specs = [pl.BlockSpec((tm, ka), lambda i, j: (i, 0)) for _ in a_list]
    in_specs += [pl.BlockSpec((None, ka, tn),
                              functools.partial(lambda i, j, r, c: (layer, r, j + c), r=r, c=col_off // tn))
                 for r in range(n_a)]
    args = list(a_list) + [w] * n_a
    if res is not None:
        in_specs.append(pl.BlockSpec((tm, tn), lambda i, j: (i, j)))
        args.append(res)
    return pl.pallas_call(
        functools.partial(_mm_kernel, n_a=n_a, has_res=res is not None),
        out_shape=jax.ShapeDtypeStruct((m, n_out), out_dtype),
        grid=(m // tm, n_out // tn),
        in_specs=in_specs,
        out_specs=pl.BlockSpec((tm, tn), lambda i, j: (i, j)),
        compiler_params=_params(("parallel", "arbitrary")),
        name=name,
    )(*args)


def _attn_kernel(slopes_ref, lam_ref, q_ref, k_ref, v_ref, hn_ref, o_ref, *, tq, tk, dk, lam_init):
    h = pl.program_id(1)
    qi = pl.program_id(2)
    slope = slopes_ref[h]
    q0 = qi * tq
    dv = q_ref.shape[1]

    lam_p = lam_ref[...]
    lam = (jnp.exp(jnp.sum(lam_p[0:1] * lam_p[1:2], keepdims=True))
           - jnp.exp(jnp.sum(lam_p[2:3] * lam_p[3:4], keepdims=True)) + lam_init)

    q = q_ref[...] * (dk ** -0.5)
    lane = _iota((tq, dv), 1)
    zero = jnp.zeros_like(q)
    q2 = jnp.concatenate([jnp.where(lane < dk, q, zero), jnp.where(lane >= dk, q, zero)], axis=0)

    def blk(i):
        return pl.multiple_of(i * tk, tk)

    def scores(k0):
        return _dot_nt(q2, k_ref[pl.ds(k0, tk), :])

    def update(s, k0, masked, carry):
        m, l, acc = carry
        v = v_ref[pl.ds(k0, tk), :]
        kpos = k0 + _iota((1, tk), 1)
        if masked:
            r = _iota((2 * tq, 1), 0)
            r = jnp.where(r >= tq, r - tq, r)
            qpos = q0 + r
            bias = slope * (r - jnp.abs(qpos - kpos)).astype(F32)
            shift = MASK_CHUNK.bit_length() - 1
            allowed = jnp.right_shift(kpos, shift) <= jnp.right_shift(qpos, shift)
            s = jnp.where(allowed, s + bias, -jnp.inf)
        else:
            s = s + slope * (kpos - q0).astype(F32)
        m_new = jnp.maximum(m, jnp.max(s, axis=-1, keepdims=True))
        alpha = jnp.exp(m - m_new)
        p = jnp.exp(s - m_new)
        l = alpha * l + jnp.sum(p, axis=-1, keepdims=True)
        acc = alpha * acc + _dot(p.astype(BF16), v)
        return m_new, l, acc

    carry = (jnp.full((2 * tq, 1), -jnp.inf, F32), jnp.zeros((2 * tq, 1), F32), jnp.zeros((2 * tq, dv), F32))
    def pair(j, c):
        ka, kb = blk(2 * j), blk(2 * j + 1)
        sa, sb = scores(ka), scores(kb)
        return update(sb, kb, False, update(sa, ka, False, c))

    def tail_odd(c):
        ka, kb = blk(qi - 1), blk(qi)
        sa, sb = scores(ka), scores(kb)
        return update(sb, kb, True, update(sa, ka, False, c))

    def tail_even(c):
        return update(scores(blk(qi)), blk(qi), True, c)

    carry = lax.fori_loop(0, jnp.right_shift(qi, 1), pair, carry)
    carry = lax.cond((qi & 1) == 1, tail_odd, tail_even, carry)
    _, l, acc = carry
    o = acc / l
    o = o[:tq] - lam * o[tq:]
    ms = jnp.mean(o * o, axis=-1, keepdims=True)
    o = o * lax.rsqrt(ms + HEAD_NORM_EPS) * hn_ref[...]
    o_ref[...] = (o * (1.0 - lam_init)).astype(o_ref.dtype)


def _diff_attention(proj, lam_p, head_norm, slopes, b, s, n_heads, lam_init):
    t = proj.shape[0]
    dv = head_norm.shape[-1]
    dk = lam_p.shape[-1]
    tq = tk = min(ATTN_BLOCK, s)
    assert s % tq == 0 and tq % MASK_CHUNK == 0
    nq = s // tq
    return pl.pallas_call(
        functools.partial(_attn_kernel, tq=tq, tk=tk, dk=dk, lam_init=lam_init),
        out_shape=jax.ShapeDtypeStruct((t, n_heads * dv), BF16),
        grid=(b, n_heads, nq),
        in_specs=[
            pl.BlockSpec(memory_space=pltpu.SMEM),
            pl.BlockSpec(lam_p.shape, lambda bi, h, qi: (0, 0)),
            pl.BlockSpec((tq, dv), lambda bi, h, qi: (bi * nq + qi, h)),
            pl.BlockSpec((s, dv), lambda bi, h, qi: (bi, n_heads + h)),
            pl.BlockSpec((s, dv), lambda bi, h, qi: (bi, 2 * n_heads + h)),
            pl.BlockSpec((1, dv), lambda bi, h, qi: (0, 0)),
        ],
        out_specs=pl.BlockSpec((tq, dv), lambda bi, h, qi: (bi * nq + qi, h)),
        compiler_params=_params(("parallel", "parallel", "arbitrary")),
        name="diff_attention",
    )(slopes, lam_p, proj, proj, proj, head_norm.reshape(1, dv))


def _rwkv_chunk_kernel(zr_ref, zk_ref, zv_ref, zl_ref, hr_ref, hk_ref, hv_ref, hl_ref,
                       mur_ref, muk_ref, muv_ref, mul_ref, w0_ref, a0_ref, kk_ref, ka_ref, rk_ref,
                       wup_ref, aup_ref, gup_ref,
                       m_ref, g_ref, rp_ref, y0_ref, bonus_ref, gate_ref,
                       sr, sk, sv, sl, *, tc, n_head):
    c = RWKV_CHUNK
    n_chunks = tc // c
    first = pl.program_id(2) == 0
    w2 = 2 * n_head
    pt = 2 * c

    def shifted(z_ref, h_ref, mu_ref, out_ref):
        z = z_ref[...]
        prev_row = jnp.where(first, 0.0, h_ref[7:8, :])
        zp = pltpu.roll(z, 1, axis=0)
        zp = jnp.where(_iota(z.shape, 0) == 0, prev_row, zp)
        out_ref[...] = z + mu_ref[...] * (zp - z)

    shifted(zr_ref, hr_ref, mur_ref, sr)
    shifted(zk_ref, hk_ref, muk_ref, sk)
    shifted(zv_ref, hv_ref, muv_ref, sv)
    shifted(zl_ref, hl_ref, mul_ref, sl)

    grp = min(RWKV_GROUP, n_chunks)
    n = grp * c
    m0 = _iota((c, w2), 1) < n_head
    m0n = _iota((n, w2), 1) < n_head
    row_p, col_p = _iota((pt, pt), 0), _iota((pt, pt), 1)
    same_head = (row_p // c) == (col_p // c)
    strict = same_head & ((row_p % c) > (col_p % c))
    incl = same_head & ((row_p % c) >= (col_p % c))
    eye_p = (row_p == col_p).astype(F32)
    row_l, col_l = _iota((w2, w2), 0), _iota((w2, w2), 1)
    head_blk = (row_l // n_head) == (col_l // n_head)
    seg_ones = head_blk.astype(BF16)
    row_n, col_n = _iota((n, n), 0), _iota((n, n), 1)
    tri = (((row_n // c) == (col_n // c)) & (row_n >= col_n)).astype(BF16)
    w_cat = jnp.concatenate([wup_ref[...], aup_ref[...]], axis=0)
    g_up = gup_ref[...].astype(BF16)

    def seg_sum(x):
        hi, lo = _split2(x)
        return _dot(hi, seg_ones) + _dot(lo, seg_ones)

    def bmm(a, b):
        return lax.dot_general(a, b, (((2,), (1,)), ((0,), (0,))), preferred_element_type=F32)

    def bmm_nt(a, b):
        return lax.dot_general(a, b, (((2,), (2,)), ((0,), (0,))), preferred_element_type=F32)

    def bmm_tn(a, b):
        return lax.dot_general(a, b, (((1,), (1,)), ((0,), (0,))), preferred_element_type=F32)

    def by_chunk(x):
        return x.reshape(grp, c, x.shape[-1])

    def stack(x):
        z = jnp.zeros_like(x)
        return jnp.concatenate([jnp.where(m0, x, z), jnp.where(m0, z, x)], axis=1)

    def unstack(x):
        return x[:, :c] + x[:, c:]

    def chunk_group(gi, _):
        rows = pl.ds(pl.multiple_of(gi * n, n), n)
        r, k, v, lora = sr[rows, :], sk[rows, :], sv[rows, :], sl[rows, :]
        d_wa = lora[:, :LORA_W + LORA_A]
        zero = jnp.zeros_like(d_wa)
        logit_w = w0_ref[...] + _dot_x3(jnp.where(m0n, jnp.tanh(d_wa), zero), w_cat)
        logit_a = a0_ref[...] + _dot_x3(jnp.where(m0n, zero, d_wa), w_cat)
        logw = -DECAY_SCALE * _sigmoid(logit_w)
        a = _sigmoid(logit_a)
        gate = _dot(_sigmoid(lora[:, LORA_W + LORA_A:]).astype(BF16), g_up)
        kk = k * kk_ref[...]
        kap = kk / jnp.maximum(jnp.sqrt(seg_sum(kk * kk)), 1e-12)
        kh = k * (1.0 + (a - 1.0) * ka_ref[...])
        bvec = kap * a
        bonus = seg_sum(r * kh * rk_ref[...]) * v

        lw_h, lw_m, lw_l = _split3(logw)
        cum = _dot(tri, lw_h) + _dot(tri, lw_m) + _dot(tri, lw_l)
        cum3 = by_chunk(cum)
        cum_end = cum3[:, c - 1:c, :]
        inv_dec = jnp.exp(-cum)
        dec_out = jnp.exp(cum_end - cum3)
        kap_t = stack(by_chunk(kap * jnp.exp(cum - logw)))
        r_t = stack(by_chunk(r * jnp.exp(cum)))
        v3 = by_chunk(v)
        v_s = stack(v3).astype(BF16)

        lhs = jnp.concatenate([kap_t, r_t], axis=1).astype(BF16)
        rhs = jnp.concatenate([stack(by_chunk(bvec * inv_dec)), stack(by_chunk(kh * inv_dec))],
                              axis=1).astype(BF16)
        aa = bmm_nt(lhs, rhs)
        a_kb = jnp.where(strict, aa[:, :pt, :pt], 0.0)
        a_kk = jnp.where(strict, aa[:, :pt, pt:], 0.0)
        a_rb = jnp.where(incl, aa[:, pt:, :pt], 0.0).astype(BF16)
        a_rk = jnp.where(incl, aa[:, pt:, pt:], 0.0).astype(BF16)

        x = a_kb
        tinv = eye_p - x
        for _ in range(int(math.log2(c)) - 1):
            xb = x.astype(BF16)
            x = bmm(xb, xb)
            tinv = bmm(tinv.astype(BF16), (eye_p + x).astype(BF16))

        akv = bmm(a_kk.astype(BF16), v_s)
        pq = bmm(tinv.astype(BF16), jnp.concatenate([kap_t, akv], axis=2).astype(BF16))
        arb_pq = bmm(a_rb, pq.astype(BF16))
        rp = r_t - arb_pq[:, :, :w2]
        y0 = bmm(a_rk, v_s) - arb_pq[:, :, w2:]
        pq_s = unstack(pq).astype(BF16)
        b_out = (by_chunk(bvec) * dec_out).astype(BF16)
        k_out = (by_chunk(kh) * dec_out).astype(BF16)
        bt_pq = bmm_tn(b_out, pq_s)
        m_mat = eye_p[:w2, :w2] * jnp.exp(cum_end) - jnp.where(head_blk, bt_pq[:, :, :w2], 0.0)
        g_mat = jnp.where(head_blk, bmm_tn(k_out, v3.astype(BF16)) - bt_pq[:, :, w2:], 0.0)

        chunks = pl.ds(gi * grp, grp)
        m_ref[chunks] = m_mat
        g_ref[chunks] = g_mat
        rp_ref[rows, :] = unstack(rp).reshape(n, w2)
        y0_ref[rows, :] = unstack(y0).reshape(n, w2)
        bonus_ref[rows, :] = bonus
        gate_ref[rows, :] = gate
        return 0

    lax.fori_loop(0, n_chunks // grp, chunk_group, 0)


def _rwkv_state_kernel(m_ref, g_ref, rp_ref, y0_ref, bonus_ref, gate_ref, lnw_ref, lnb_ref, o_ref, st_ref,
                       *, n_pairs_blk, n_head):
    @pl.when(pl.program_id(2) == 0)
    def _():
        st_ref[...] = jnp.zeros_like(st_ref)

    w2 = 2 * n_head
    head_blk = ((_iota((w2, w2), 0) // n_head) == (_iota((w2, w2), 1) // n_head)).astype(BF16)

    def seg_mean(x):
        hi, lo = _split2(x)
        return (_dot(hi, head_blk) + _dot(lo, head_blk)) * (1.0 / n_head)

    def bmm(a, b):
        return lax.dot_general(a, b, (((2,), (1,)), ((0,), (0,))), preferred_element_type=F32)

    ch = rp_ref.shape[1]
    st_hi, st_lo = _split2(st_ref[...])
    rp = rp_ref[...].astype(BF16)
    y = bmm(rp, st_hi) + bmm(rp, st_lo) + y0_ref[...]
    m = m_ref[:, 0].astype(BF16)
    st_ref[...] = bmm(m, st_hi) + bmm(m, st_lo) + g_ref[:, 0]
    y = y.reshape(n_pairs_blk * ch, w2)
    mean = seg_mean(y)
    yc = y - mean
    var = seg_mean(yc * yc)
    yn = (yc * lax.rsqrt(var + GN_EPS)).reshape(n_pairs_blk, ch, w2)
    out = (yn * lnw_ref[...] + lnb_ref[...] + bonus_ref[...]) * gate_ref[...]
    for p in range(n_pairs_blk):
        o_ref[:, p * w2:(p + 1) * w2] = out[p].astype(o_ref.dtype)


def _rwkv7(z, mu, w0, w_up, a0, a_up, g_up, k_k, k_a, r_k, ln_w, ln_b, b, s, tc=512, pairs_blk=16):
    t = z.shape[0]
    c = w0.shape[-1]
    n_head = r_k.shape[-1]
    w2 = 2 * n_head
    assert w2 == LANES and (3 * c) % (2 * LANES) == 0 and LORA_W + LORA_A == LANES
    n_pairs = c // w2
    tc = min(tc, s)
    pairs_blk = min(pairs_blk, n_pairs)
    n_tb = s // tc
    ch = RWKV_CHUNK
    n_chunks = s // ch
    cpb = tc // ch
    lw = LORA_W + LORA_A + LORA_G
    cb = c // w2
    lb = 3 * c // lw
    mu2 = mu.reshape(1, -1)
    row = lambda v_: v_.reshape(1, c)

    def zspec(col_blk_fn, width):
        return pl.BlockSpec((tc, width), lambda bi, p, ti: (bi * n_tb + ti, col_blk_fn(p)))

    def hspec(col_blk_fn, width):
        return pl.BlockSpec((8, width),
                            lambda bi, p, ti: (jnp.maximum((bi * n_tb + ti) * (tc // 8) - 1, 0), col_blk_fn(p)))

    def pspec(col_blk_fn, width, rows=1):
        return pl.BlockSpec((rows, width), lambda bi, p, ti: (0, col_blk_fn(p)))

    sec = [lambda p: p, lambda p: cb + p, lambda p: 2 * cb + p]
    in_specs = ([zspec(f, w2) for f in sec] + [zspec(lambda p: lb, lw)]
                + [hspec(f, w2) for f in sec] + [hspec(lambda p: lb, lw)]
                + [pspec(f, w2) for f in sec] + [pspec(lambda p: lb, lw)]
                + [pspec(sec[0], w2) for _ in range(5)]
                + [pspec(sec[0], w2, LORA_W), pspec(sec[0], w2, LORA_A), pspec(sec[0], w2, LORA_G)])
    mat_shape = jax.ShapeDtypeStruct((b * n_pairs, n_chunks, w2, w2), F32)
    slab_shape = jax.ShapeDtypeStruct((n_pairs, t, w2), F32)
    mat_spec = pl.BlockSpec((None, cpb, w2, w2), lambda bi, p, ti: (bi * n_pairs + p, ti, 0, 0))
    slab_spec = pl.BlockSpec((None, tc, w2), lambda bi, p, ti: (p, bi * n_tb + ti, 0))
    m_mat, g_mat, rp, y0, bonus, gate = pl.pallas_call(
        functools.partial(_rwkv_chunk_kernel, tc=tc, n_head=n_head),
        out_shape=[mat_shape, mat_shape, slab_shape, slab_shape, slab_shape, slab_shape],
        grid=(b, n_pairs, n_tb),
        in_specs=in_specs,
        out_specs=[mat_spec, mat_spec, slab_spec, slab_spec, slab_spec, slab_spec],
        scratch_shapes=[pltpu.VMEM((tc, w2), F32)] * 3 + [pltpu.VMEM((tc, lw), F32)],
        compiler_params=_params(("parallel", "parallel", "arbitrary")),
        name="rwkv_chunk",
    )(z, z, z, z, z, z, z, z, mu2, mu2, mu2, mu2, row(w0), row(a0), row(k_k), row(k_a), row(r_k),
      w_up, a_up, g_up)

    n_pb = n_pairs // pairs_blk
    wblk = pairs_blk * w2
    mat_in = pl.BlockSpec((pairs_blk, 1, w2, w2), lambda bi, pb, ci: (bi * n_pb + pb, ci, 0, 0))
    slab_in = pl.BlockSpec((pairs_blk, ch, w2), lambda bi, pb, ci: (pb, bi * n_chunks + ci, 0))
    par_in = pl.BlockSpec((pairs_blk, 1, w2), lambda bi, pb, ci: (pb, 0, 0))
    by_pair = lambda v_: v_.reshape(n_pairs, 1, w2)
    return pl.pallas_call(
        functools.partial(_rwkv_state_kernel, n_pairs_blk=pairs_blk, n_head=n_head),
        out_shape=jax.ShapeDtypeStruct((t, c), BF16),
        grid=(b, n_pb, n_chunks),
        in_specs=[mat_in, mat_in, slab_in, slab_in, slab_in, slab_in, par_in, par_in],
        out_specs=pl.BlockSpec((ch, wblk), lambda bi, pb, ci: (bi * n_chunks + ci, pb)),
        scratch_shapes=[pltpu.VMEM((pairs_blk, w2, w2), F32)],
        compiler_params=_params(("parallel", "parallel", "arbitrary")),
        name="rwkv_state",
    )(m_mat, g_mat, rp, y0, bonus, gate, by_pair(ln_w), by_pair(ln_b))


def _router_kernel(x_ref, g_ref, w_ref, b_ref, h_ref, logit_ref, *, eps):
    x = x_ref[...]
    ms = jnp.mean(x * x, axis=-1, keepdims=True)
    h = x * lax.rsqrt(ms + eps) * g_ref[...]
    h_ref[...] = h
    logit_ref[...] = _dot_x3(h, w_ref[...]) + b_ref[...]


def _router(x, g, w_router, b_router, tm=256):
    t, d = x.shape
    n = w_router.shape[1]
    return pl.pallas_call(
        functools.partial(_router_kernel, eps=RMS_EPS),
        out_shape=[jax.ShapeDtypeStruct((t, d), F32), jax.ShapeDtypeStruct((t, n), F32)],
        grid=(t // tm,),
        in_specs=[pl.BlockSpec((tm, d), lambda i: (i, 0)), pl.BlockSpec((1, d), lambda i: (0, 0)),
                  pl.BlockSpec((d, n), lambda i: (0, 0)), pl.BlockSpec((1, n), lambda i: (0, 0))],
        out_specs=[pl.BlockSpec((tm, d), lambda i: (i, 0)), pl.BlockSpec((tm, n), lambda i: (i, 0))],
        compiler_params=_params(("parallel",)),
        name="moe_router",
    )(x, g.reshape(1, d), w_router, b_router)


def _gather_rows(idx_ref, base, src_hbm, dst_vmem, sem, n_rows):
    def issue(r, _):
        pltpu.make_async_copy(src_hbm.at[pl.ds(idx_ref[base + r], 1), :], dst_vmem.at[pl.ds(r, 1), :], sem).start()
        return 0
    lax.fori_loop(0, n_rows, issue, 0, unroll=8)


def _wait_rows(src_hbm, dst_vmem, sem, n_rows):
    pltpu.make_async_copy(src_hbm.at[pl.ds(0, n_rows), :], dst_vmem, sem).wait()


def _dispatch_kernel(tok_ref, n_used_ref, h_hbm, o_ref, buf, sem, *, blk):
    used = pl.program_id(0) < n_used_ref[0]

    @pl.when(used)
    def _():
        base = pl.program_id(0) * blk
        _gather_rows(tok_ref, base, h_hbm, buf, sem, blk)
        _wait_rows(h_hbm, buf, sem, blk)
        o_ref[...] = buf[...].astype(o_ref.dtype)

    @pl.when(jnp.logical_not(used))
    def _():
        o_ref[...] = jnp.zeros_like(o_ref)


def _dispatch(slot_tok, n_used, h, blk):
    n_slots = slot_tok.shape[0]
    d = h.shape[1]
    return pl.pallas_call(
        functools.partial(_dispatch_kernel, blk=blk),
        out_shape=jax.ShapeDtypeStruct((n_slots, d), BF16),
        grid_spec=pltpu.PrefetchScalarGridSpec(
            num_scalar_prefetch=2,
            grid=(n_slots // blk,),
            in_specs=[pl.BlockSpec(memory_space=pl.ANY)],
            out_specs=pl.BlockSpec((blk, d), lambda i, tok, nu: (i, 0)),
            scratch_shapes=[pltpu.VMEM((blk, d), F32), pltpu.SemaphoreType.DMA(())],
        ),
        compiler_params=_params(("arbitrary",)),
        name="moe_dispatch",
    )(slot_tok, n_used, h)


def _expert_up_kernel(blk_e_ref, n_used_ref, x_ref, wg_ref, wu_ref, o_ref):
    used = pl.program_id(1) < n_used_ref[0]

    @pl.when(used)
    def _():
        x = x_ref[...]
        gate = _dot(x, wg_ref[...].astype(BF16))
        up = _dot(x, wu_ref[...].astype(BF16))
        o_ref[...] = (gate * _sigmoid(gate) * up).astype(o_ref.dtype)

    @pl.when(jnp.logical_not(used))
    def _():
        o_ref[...] = jnp.zeros_like(o_ref)


def _expert_down_kernel(blk_e_ref, n_used_ref, h_ref, wd_ref, o_ref):
    used = pl.program_id(0) < n_used_ref[0]

    @pl.when(used)
    def _():
        o_ref[...] = _dot(h_ref[...], wd_ref[...].astype(BF16))

    @pl.when(jnp.logical_not(used))
    def _():
        o_ref[...] = jnp.zeros_like(o_ref)


def _experts(blk_e, n_used, xs, wg, wu, wd, layer, blk, tn=256):
    n_slots, d = xs.shape
    de = wg.shape[-1]
    n_blk = n_slots // blk
    tn = min(tn, de)
    used = lambda i, nu: jnp.minimum(i, nu[0] - 1)
    w_up_spec = pl.BlockSpec((None, None, d, tn), lambda j, i, be, nu: (layer, be[used(i, nu)], 0, j))
    hid = pl.pallas_call(
        _expert_up_kernel,
        out_shape=jax.ShapeDtypeStruct((n_slots, de), BF16),
        grid_spec=pltpu.PrefetchScalarGridSpec(
            num_scalar_prefetch=2,
            grid=(de // tn, n_blk),
            in_specs=[pl.BlockSpec((blk, d), lambda j, i, be, nu: (used(i, nu), 0)), w_up_spec, w_up_spec],
            out_specs=pl.BlockSpec((blk, tn), lambda j, i, be, nu: (i, j)),
        ),
        compiler_params=_params(("parallel", "arbitrary")),
        name="moe_expert_up",
    )(blk_e, n_used, xs, wg, wu)
    return pl.pallas_call(
        _expert_down_kernel,
        out_shape=jax.ShapeDtypeStruct((n_slots, d), F32),
        grid_spec=pltpu.PrefetchScalarGridSpec(
            num_scalar_prefetch=2,
            grid=(n_blk,),
            in_specs=[pl.BlockSpec((blk, de), lambda i, be, nu: (used(i, nu), 0)),
                      pl.BlockSpec((None, None, de, d), lambda i, be, nu: (layer, be[used(i, nu)], 0, 0))],
            out_specs=pl.BlockSpec((blk, d), lambda i, be, nu: (i, 0)),
        ),
        compiler_params=_params(("arbitrary",)),
        name="moe_expert_down",
    )(blk_e, n_used, hid, wd)


def _combine_kernel(pos0_ref, pos1_ref, x_ref, w_ref, y_hbm, o_ref, buf0, buf1, sem0, sem1, *, tb):
    base = pl.program_id(0) * tb
    _gather_rows(pos0_ref, base, y_hbm, buf0, sem0, tb)
    _gather_rows(pos1_ref, base, y_hbm, buf1, sem1, tb)
    w = w_ref[...]
    _wait_rows(y_hbm, buf0, sem0, tb)
    _wait_rows(y_hbm, buf1, sem1, tb)
    o_ref[...] = x_ref[...] + (buf0[...] * w[:, 0:1] + buf1[...] * w[:, 1:2])


def _combine(pos0, pos1, x, wts, y, tb=256):
    t, d = x.shape
    return pl.pallas_call(
        functools.partial(_combine_kernel, tb=tb),
        out_shape=jax.ShapeDtypeStruct((t, d), F32),
        grid_spec=pltpu.PrefetchScalarGridSpec(
            num_scalar_prefetch=2,
            grid=(t // tb,),
            in_specs=[pl.BlockSpec((tb, d), lambda i, p0, p1: (i, 0)),
                      pl.BlockSpec((tb, TOP_K), lambda i, p0, p1: (i, 0)),
                      pl.BlockSpec(memory_space=pl.ANY)],
            out_specs=pl.BlockSpec((tb, d), lambda i, p0, p1: (i, 0)),
            scratch_shapes=[pltpu.VMEM((tb, d), F32), pltpu.VMEM((tb, d), F32),
                            pltpu.SemaphoreType.DMA(()), pltpu.SemaphoreType.DMA(())],
        ),
        compiler_params=_params(("arbitrary",)),
        name="moe_combine",
    )(pos0, pos1, x, wts, y)


def _route(logits, n_groups, n_experts, blk):
    t = logits.shape[0]
    epg = n_experts // n_groups
    g_logits = logits[:, :n_groups]
    p_group = jax.nn.softmax(g_logits, axis=-1)
    g_idx = jnp.argmax(g_logits, axis=-1)
    p_sel = jnp.take_along_axis(p_group, g_idx[:, None], axis=1)
    e_logits = logits[:, n_groups:n_groups + n_experts].reshape(t, n_groups, epg)
    e_sel = jnp.take_along_axis(e_logits, g_idx[:, None, None], axis=1)[:, 0]
    top_v, top_i = lax.top_k(e_sel, TOP_K)
    wts = jax.nn.softmax(top_v, axis=-1) * p_sel
    eid = (g_idx[:, None] * epg + top_i).astype(jnp.int32)
    e_flat = eid.reshape(-1)
    n_assign = e_flat.shape[0]
    onehot = (e_flat[:, None] == jnp.arange(n_experts, dtype=jnp.int32)[None, :]).astype(jnp.int32)
    csum = jnp.cumsum(onehot, axis=0)
    counts = csum[-1]
    rank = jnp.take_along_axis(csum, e_flat[:, None], axis=1)[:, 0] - 1
    padded = (counts + blk - 1) // blk * blk
    pends = jnp.cumsum(padded)
    pstarts = pends - padded
    pos = (pstarts[e_flat] + rank).astype(jnp.int32)
    n_blk = (n_assign + blk - 1) // blk + n_experts
    tok_flat = jnp.repeat(jnp.arange(t, dtype=jnp.int32), TOP_K)
    slot_tok = jnp.zeros((n_blk * blk,), jnp.int32).at[pos].set(tok_flat)
    blk_start = jnp.arange(n_blk, dtype=jnp.int32) * blk
    blk_e = jnp.minimum(jnp.sum(blk_start[:, None] >= pends[None, :], axis=1), n_experts - 1).astype(jnp.int32)
    pos2 = pos.reshape(t, TOP_K)
    n_used = (pends[-1:] // blk).astype(jnp.int32)
    return slot_tok, blk_e, n_used, pos2[:, 0], pos2[:, 1], wts.astype(F32)


def _hier_moe(x, ffn_norm, w_rg, b_rg, w_re, b_re, wg, wu, wd, layer):
    n_groups, n_experts = w_rg.shape[1], w_re.shape[1]
    n_r = n_groups + n_experts
    n_pad = -n_r % LANES
    w_router = jnp.pad(jnp.concatenate([w_rg, w_re], axis=1), ((0, 0), (0, n_pad)))
    b_router = jnp.pad(jnp.concatenate([b_rg, b_re], axis=0), (0, n_pad)).reshape(1, -1)
    h, logits = _router(x, ffn_norm, w_router, b_router)
    slot_tok, blk_e, n_used, pos0, pos1, wts = _route(logits, n_groups, n_experts, MOE_BLK)
    xs = _dispatch(slot_tok, n_used, h, MOE_BLK)
    y = _experts(blk_e, n_used, xs, wg, wu, wd, layer, MOE_BLK)
    return _combine(pos0, pos1, x, wts, y)


def kernel(x, attn_norm, w_in, diff_lambda, diff_head_norm, rwkv_mu, rwkv_w0, rwkv_w_up, rwkv_a0, rwkv_a_up, rwkv_g_up, rwkv_k_k, rwkv_k_a, rwkv_r_k, rwkv_ln_w, rwkv_ln_b, w_out, ffn_norm, router_group, router_group_bias, router_expert, router_expert_bias, expert_w_gate, expert_w_up, expert_w_down, final_norm):
    b, s, d = x.shape
    depth = w_in.shape[0]
    t = b * s
    dv = diff_head_norm.shape[-1]
    rw = rwkv_w0.shape[-1]
    dw = d - rw
    n_heads = dw // dv
    attn_cols = 3 * dw
    rwkv_cols = w_in.shape[-1] - attn_cols
    slopes = jnp.asarray([2.0 ** (-8.0 * (i + 1) / n_heads) for i in range(n_heads)], F32)
    tm = min(1024, t)

    xt = x.reshape(t, d)
    for l in range(depth):
        lam_init = 0.8 - 0.6 * math.exp(-0.3 * l)
        h = _rmsnorm(xt, attn_norm[l], BF16)
        proj_a = _matmul([h], w_in, l, 0, attn_cols, BF16, tm, _tile(attn_cols, (512, 256, 128)), name="in_proj_attn")
        proj_r = _matmul([h], w_in, l, attn_cols, rwkv_cols, F32, tm, 256, name="in_proj_rwkv")
        y_diff = _diff_attention(proj_a, diff_lambda[l], diff_head_norm[l], slopes, b, s, n_heads, lam_init)
        y_rwkv = _rwkv7(proj_r, rwkv_mu[l], rwkv_w0[l], rwkv_w_up[l], rwkv_a0[l], rwkv_a_up[l], rwkv_g_up[l],
                        rwkv_k_k[l], rwkv_k_a[l], rwkv_r_k[l], rwkv_ln_w[l], rwkv_ln_b[l], b, s)
        xt = _matmul([y_diff, y_rwkv], w_out, l, 0, d, F32, tm, _tile(d, (512, 256, 128)), res=xt, name="out_proj")
        xt = _hier_moe(xt, ffn_norm[l], router_group[l], router_group_bias[l], router_expert[l],
                       router_expert_bias[l], expert_w_gate, expert_w_up, expert_w_down, l)
    return _rmsnorm(xt, final_norm, F32).reshape(b, s, d)
```

```python
import functools
import math

import jax
import jax.numpy as jnp
from jax import lax
from jax.experimental import pallas as pl
from jax.experimental.pallas import tpu as pltpu

F32 = jnp.float32
BF16 = jnp.bfloat16

LANES = 128
VMEM_BUDGET_BYTES = 56 << 20

MASK_CHUNK = 64
RMS_EPS = 1e-6
HEAD_NORM_EPS = 1e-5
GN_EPS = 64e-5
DECAY_SCALE = math.exp(-0.5)
TOP_K = 2
LORA_W = 64
LORA_A = 64
LORA_G = 128

ATTN_BLOCK = 512
LOG2E = math.log2(math.e)
RWKV_CHUNK = 64
RWKV_GROUP = 8
MOE_BLK = 256
DMA_ISSUE_UNROLL = 8


def _tile(n, candidates):
    return next(c for c in candidates if n % c == 0)


def _params(semantics):
    return pltpu.CompilerParams(dimension_semantics=semantics, vmem_limit_bytes=VMEM_BUDGET_BYTES)


def _dot(a, b):
    return jnp.dot(a, b, preferred_element_type=F32)


def _dot_nt(a, b):
    return lax.dot_general(a, b, (((1,), (1,)), ((), ())), preferred_element_type=F32)


def _dot_tn(a, b):
    return lax.dot_general(a, b, (((0,), (0,)), ((), ())), preferred_element_type=F32)


def _split2(x):
    hi = x.astype(BF16)
    lo = (x - hi.astype(F32)).astype(BF16)
    return hi, lo


def _split3(x):
    hi = x.astype(BF16)
    r1 = x - hi.astype(F32)
    mid = r1.astype(BF16)
    lo = (r1 - mid.astype(F32)).astype(BF16)
    return hi, mid, lo


def _dot_x3(a, b):
    ah, al = _split2(a)
    bh, bl = _split2(b)
    return _dot(ah, bh) + _dot(ah, bl) + _dot(al, bh)


def _iota(shape, dim):
    return lax.broadcasted_iota(jnp.int32, shape, dim)


def _sigmoid(x):
    return 1.0 / (1.0 + jnp.exp(-x))


def _rmsnorm_kernel(x_ref, g_ref, o_ref, *, eps):
    x = x_ref[...]
    ms = jnp.mean(x * x, axis=-1, keepdims=True)
    o_ref[...] = (x * lax.rsqrt(ms + eps) * g_ref[...]).astype(o_ref.dtype)


def _rmsnorm(x, g, out_dtype, tm=256):
    t, d = x.shape
    return pl.pallas_call(
        functools.partial(_rmsnorm_kernel, eps=RMS_EPS),
        out_shape=jax.ShapeDtypeStruct((t, d), out_dtype),
        grid=(t // tm,),
        in_specs=[pl.BlockSpec((tm, d), lambda i: (i, 0)), pl.BlockSpec((1, d), lambda i: (0, 0))],
        out_specs=pl.BlockSpec((tm, d), lambda i: (i, 0)),
        compiler_params=_params(("parallel",)),
        name="rmsnorm",
    )(x, g.reshape(1, d))


def _mm_kernel(*refs, n_a, has_res):
    a_refs, w_refs = refs[:n_a], refs[n_a:2 * n_a]
    o_ref = refs[-1]
    acc = None
    for a_ref, w_ref in zip(a_refs, w_refs):
        d = _dot(a_ref[...], w_ref[...].astype(BF16))
        acc = d if acc is None else acc + d
    if has_res:
        acc = acc + refs[2 * n_a][...]
    o_ref[...] = acc.astype(o_ref.dtype)


def _matmul(a_list, w, layer, col_off, n_out, out_dtype, tm, tn, res=None, name="matmul"):
    m, ka = a_list[0].shape
    n_a = len(a_list)
    assert w.shape[1] == ka * n_a and col_off % tn == 0 and n_out % tn == 0 and m % tm == 0
    in_specs = [pl.BlockSpec((tm, ka), lambda i, j: (i, 0)) for _ in a_list]
    in_specs += [pl.BlockSpec((None, ka, tn),
                              functools.partial(lambda i, j, r, c: (layer, r, j + c), r=r, c=col_off // tn))
                 for r in range(n_a)]
    args = list(a_list) + [w] * n_a
    if res is not None:
        in_specs.append(pl.BlockSpec((tm, tn), lambda i, j: (i, j)))
        args.append(res)
    return pl.pallas_call(
        functools.partial(_mm_kernel, n_a=n_a, has_res=res is not None),
        out_shape=jax.ShapeDtypeStruct((m, n_out), out_dtype),
        grid=(m // tm, n_out // tn),
        in_specs=in_specs,
        out_specs=pl.BlockSpec((tm, tn), lambda i, j: (i, j)),
        compiler_params=_params(("parallel", "arbitrary")),
        name=name,
    )(*args)


def _attn_kernel(slopes_ref, lam_ref, q_ref, k_ref, v_ref, hn_ref, o_ref, *, tq, tk, dk, lam_init):
    h = pl.program_id(1)
    qi = pl.program_id(2)
    slope = slopes_ref[h]
    q0 = qi * tq
    dv = q_ref.shape[1]

    lam_p = lam_ref[...]
    lam = (jnp.exp(jnp.sum(lam_p[0:1] * lam_p[1:2], keepdims=True))
           - jnp.exp(jnp.sum(lam_p[2:3] * lam_p[3:4], keepdims=True)) + lam_init)

    q = q_ref[...] * (dk ** -0.5 * LOG2E)
    lane = _iota((tq, dv), 1)
    zero = jnp.zeros_like(q)
    q2 = jnp.concatenate([jnp.where(lane < dk, q, zero), jnp.where(lane >= dk, q, zero)], axis=0)
    shift = MASK_CHUNK.bit_length() - 1

    def blk(i):
        return pl.multiple_of(i * tk, tk)

    def scores(k0):
        return _dot_nt(q2, k_ref[pl.ds(k0, tk), :])

    def update(s, k0, masked, carry):
        v = v_ref[pl.ds(k0, tk), :]
        kpos = k0 + _iota((1, tk), 1)
        m, l, acc = carry
        if masked:
            r = _iota((2 * tq, 1), 0)
            r = jnp.where(r >= tq, r - tq, r)
            qpos = q0 + r
            bias = slope * (r - jnp.abs(qpos - kpos)).astype(F32)
            allowed = jnp.right_shift(kpos, shift) <= jnp.right_shift(qpos, shift)
            s = jnp.where(allowed, s + bias, -jnp.inf)
        else:
            s = s + slope * (kpos - q0).astype(F32)
        m_new = jnp.maximum(m, jnp.max(s, axis=-1, keepdims=True))
        alpha = jnp.exp2(m - m_new)
        p = jnp.exp2(s - m_new)
        l = alpha * l + jnp.sum(p, axis=-1, keepdims=True)
        acc = alpha * acc + _dot(p.astype(BF16), v)
        return m_new, l, acc

    carry = (jnp.full((2 * tq, 1), -jnp.inf, F32), jnp.zeros((2 * tq, 1), F32), jnp.zeros((2 * tq, dv), F32))
    def pair(j, c):
        ka, kb = blk(2 * j), blk(2 * j + 1)
        sa, sb = scores(ka), scores(kb)
        return update(sb, kb, False, update(sa, ka, False, c))

    def tail_odd(c):
        ka, kb = blk(qi - 1), blk(qi)
        sa, sb = scores(ka), scores(kb)
        return update(sb, kb, True, update(sa, ka, False, c))

    def tail_even(c):
        return update(scores(blk(qi)), blk(qi), True, c)

    carry = lax.fori_loop(0, jnp.right_shift(qi, 1), pair, carry)
    carry = lax.cond((qi & 1) == 1, tail_odd, tail_even, carry)
    _, l, acc = carry
    o = acc / l
    o = o[:tq] - lam * o[tq:]
    ms = jnp.mean(o * o, axis=-1, keepdims=True)
    o = o * lax.rsqrt(ms + HEAD_NORM_EPS) * hn_ref[...]
    o_ref[...] = (o * (1.0 - lam_init)).astype(o_ref.dtype)


def _diff_attention(proj, lam_p, head_norm, slopes, b, s, n_heads, lam_init):
    t = proj.shape[0]
    dv = head_norm.shape[-1]
    dk = lam_p.shape[-1]
    tq = tk = min(ATTN_BLOCK, s)
    assert s % tq == 0 and tq % MASK_CHUNK == 0
    nq = s // tq
    return pl.pallas_call(
        functools.partial(_attn_kernel, tq=tq, tk=tk, dk=dk, lam_init=lam_init),
        out_shape=jax.ShapeDtypeStruct((t, n_heads * dv), BF16),
        grid=(b, n_heads, nq),
        in_specs=[
            pl.BlockSpec(memory_space=pltpu.SMEM),
            pl.BlockSpec(lam_p.shape, lambda bi, h, qi: (0, 0)),
            pl.BlockSpec((tq, dv), lambda bi, h, qi: (bi * nq + qi, h)),
            pl.BlockSpec((s, dv), lambda bi, h, qi: (bi, n_heads + h)),
            pl.BlockSpec((s, dv), lambda bi, h, qi: (bi, 2 * n_heads + h)),
            pl.BlockSpec((1, dv), lambda bi, h, qi: (0, 0)),
        ],
        out_specs=pl.BlockSpec((tq, dv), lambda bi, h, qi: (bi * nq + qi, h)),
        compiler_params=_params(("parallel", "parallel", "arbitrary")),
        name="diff_attention",
    )(slopes, lam_p, proj, proj, proj, head_norm.reshape(1, dv))


def _rwkv_chunk_kernel(zr_ref, zk_ref, zv_ref, zl_ref, hr_ref, hk_ref, hv_ref, hl_ref,
                       mur_ref, muk_ref, muv_ref, mul_ref, w0_ref, a0_ref, kk_ref, ka_ref, rk_ref,
                       wup_ref, aup_ref, gup_ref,
                       m_ref, g_ref, rp_ref, y0_ref, bonus_ref, gate_ref,
                       sr, sk, sv, sl, *, tc, n_head):
    c = RWKV_CHUNK
    n_chunks = tc // c
    first = pl.program_id(2) == 0
    w2 = 2 * n_head
    pt = 2 * c

    def shifted(z_ref, h_ref, mu_ref, out_ref):
        z = z_ref[...]
        prev_row = jnp.where(first, 0.0, h_ref[7:8, :])
        zp = pltpu.roll(z, 1, axis=0)
        zp = jnp.where(_iota(z.shape, 0) == 0, prev_row, zp)
        out_ref[...] = z + mu_ref[...] * (zp - z)

    shifted(zr_ref, hr_ref, mur_ref, sr)
    shifted(zk_ref, hk_ref, muk_ref, sk)
    shifted(zv_ref, hv_ref, muv_ref, sv)
    shifted(zl_ref, hl_ref, mul_ref, sl)

    grp = min(RWKV_GROUP, n_chunks)
    n = grp * c
    m0 = _iota((c, w2), 1) < n_head
    m0n = _iota((n, w2), 1) < n_head
    row_p, col_p = _iota((pt, pt), 0), _iota((pt, pt), 1)
    same_head = (row_p // c) == (col_p // c)
    strict = same_head & ((row_p % c) > (col_p % c))
    incl = same_head & ((row_p % c) >= (col_p % c))
    eye_p = (row_p == col_p).astype(F32)
    row_l, col_l = _iota((w2, w2), 0), _iota((w2, w2), 1)
    head_blk = (row_l // n_head) == (col_l // n_head)
    seg_ones = head_blk.astype(BF16)
    row_n, col_n = _iota((n, n), 0), _iota((n, n), 1)
    tri = (((row_n // c) == (col_n // c)) & (row_n >= col_n)).astype(BF16)
    w_cat = jnp.concatenate([wup_ref[...], aup_ref[...]], axis=0)
    g_up = gup_ref[...].astype(BF16)

    def seg_sum(x):
        hi, lo = _split2(x)
        return _dot(hi, seg_ones) + _dot(lo, seg_ones)

    def bmm(a, b):
        return lax.dot_general(a, b, (((2,), (1,)), ((0,), (0,))), preferred_element_type=F32)

    def bmm_nt(a, b):
        return lax.dot_general(a, b, (((2,), (2,)), ((0,), (0,))), preferred_element_type=F32)

    def bmm_tn(a, b):
        return lax.dot_general(a, b, (((1,), (1,)), ((0,), (0,))), preferred_element_type=F32)

    def by_chunk(x):
        return x.reshape(grp, c, x.shape[-1])

    def stack(x):
        z = jnp.zeros_like(x)
        return jnp.concatenate([jnp.where(m0, x, z), jnp.where(m0, z, x)], axis=1)

    def unstack(x):
        return x[:, :c] + x[:, c:]

    def chunk_group(gi, _):
        rows = pl.ds(pl.multiple_of(gi * n, n), n)
        r, k, v, lora = sr[rows, :], sk[rows, :], sv[rows, :], sl[rows, :]
        d_wa = lora[:, :LORA_W + LORA_A]
        zero = jnp.zeros_like(d_wa)
        logit_w = w0_ref[...] + _dot_x3(jnp.where(m0n, jnp.tanh(d_wa), zero), w_cat)
        logit_a = a0_ref[...] + _dot_x3(jnp.where(m0n, zero, d_wa), w_cat)
        logw = -DECAY_SCALE * _sigmoid(logit_w)
        a = _sigmoid(logit_a)
        kk = k * kk_ref[...]
        kap = kk / jnp.maximum(jnp.sqrt(seg_sum(kk * kk)), 1e-12)
        kh = k * (1.0 + (a - 1.0) * ka_ref[...])
        bvec = kap * a

        lw_h, lw_m, lw_l = _split3(logw)
        cum = _dot(tri, lw_h) + _dot(tri, lw_m) + _dot(tri, lw_l)
        cum3 = by_chunk(cum)
        cum_end = cum3[:, c - 1:c, :]
        inv_dec = jnp.exp(-cum)
        kap_t = stack(by_chunk(kap * jnp.exp(cum - logw)))
        r_t = stack(by_chunk(r * jnp.exp(cum)))

        lhs = jnp.concatenate([kap_t, r_t], axis=1).astype(BF16)
        rhs = jnp.concatenate([stack(by_chunk(bvec * inv_dec)), stack(by_chunk(kh * inv_dec))],
                              axis=1).astype(BF16)
        aa = bmm_nt(lhs, rhs)

        v3 = by_chunk(v)
        v_s = stack(v3).astype(BF16)
        dec_out = jnp.exp(cum_end - cum3)
        b_out = (by_chunk(bvec) * dec_out).astype(BF16)
        k_out = (by_chunk(kh) * dec_out).astype(BF16)
        kv = bmm_tn(k_out, v3.astype(BF16))
        gate_ref[rows, :] = _dot(_sigmoid(lora[:, LORA_W + LORA_A:]).astype(BF16), g_up)
        bonus_ref[rows, :] = seg_sum(r * kh * rk_ref[...]) * v

        a_kb = jnp.where(strict, aa[:, :pt, :pt], 0.0)
        a_kk = jnp.where(strict, aa[:, :pt, pt:], 0.0)
        a_rb = jnp.where(incl, aa[:, pt:, :pt], 0.0).astype(BF16)
        a_rk = jnp.where(incl, aa[:, pt:, pt:], 0.0).astype(BF16)

        x = a_kb
        tinv = eye_p - x
        for _ in range(int(math.log2(c)) - 1):
            xb = x.astype(BF16)
            x = bmm(xb, xb)
            tinv = bmm(tinv.astype(BF16), (eye_p + x).astype(BF16))

        akv = bmm(a_kk.astype(BF16), v_s)
        pq = bmm(tinv.astype(BF16), jnp.concatenate([kap_t, akv], axis=2).astype(BF16))
        arb_pq = bmm(a_rb, pq.astype(BF16))
        rp = r_t - arb_pq[:, :, :w2]
        y0 = bmm(a_rk, v_s) - arb_pq[:, :, w2:]
        pq_s = unstack(pq).astype(BF16)
        bt_pq = bmm_tn(b_out, pq_s)
        m_mat = eye_p[:w2, :w2] * jnp.exp(cum_end) - jnp.where(head_blk, bt_pq[:, :, :w2], 0.0)
        g_mat = jnp.where(head_blk, kv - bt_pq[:, :, w2:], 0.0)

        chunks = pl.ds(gi * grp, grp)
        m_ref[chunks] = m_mat
        g_ref[chunks] = g_mat
        rp_ref[rows, :] = unstack(rp).reshape(n, w2)
        y0_ref[rows, :] = unstack(y0).reshape(n, w2)
        return 0

    lax.fori_loop(0, n_chunks // grp, chunk_group, 0)


def _rwkv_state_kernel(m_ref, g_ref, rp_ref, y0_ref, bonus_ref, gate_ref, lnw_ref, lnb_ref, o_ref, st_ref,
                       *, n_pairs_blk, n_head):
    @pl.when(pl.program_id(2) == 0)
    def _():
        st_ref[...] = jnp.zeros_like(st_ref)

    w2 = 2 * n_head
    head_blk = ((_iota((w2, w2), 0) // n_head) == (_iota((w2, w2), 1) // n_head)).astype(BF16)

    def seg_mean(x):
        hi, lo = _split2(x)
        return (_dot(hi, head_blk) + _dot(lo, head_blk)) * (1.0 / n_head)

    def bmm(a, b):
        return lax.dot_general(a, b, (((2,), (1,)), ((0,), (0,))), preferred_element_type=F32)

    ch = rp_ref.shape[1]
    st_hi, st_lo = _split2(st_ref[...])
    rp = rp_ref[...].astype(BF16)
    y = bmm(rp, st_hi) + bmm(rp, st_lo) + y0_ref[...]
    m = m_ref[:, 0].astype(BF16)
    st_ref[...] = bmm(m, st_hi) + bmm(m, st_lo) + g_ref[:, 0]
    y = y.reshape(n_pairs_blk * ch, w2)
    mean = seg_mean(y)
    yc = y - mean
    var = seg_mean(yc * yc)
    yn = (yc * lax.rsqrt(var + GN_EPS)).reshape(n_pairs_blk, ch, w2)
    out = (yn * lnw_ref[...] + lnb_ref[...] + bonus_ref[...]) * gate_ref[...]
    for p in range(n_pairs_blk):
        o_ref[:, p * w2:(p + 1) * w2] = out[p].astype(o_ref.dtype)


def _rwkv7(z, mu, w0, w_up, a0, a_up, g_up, k_k, k_a, r_k, ln_w, ln_b, b, s, tc=512, pairs_blk=16):
    t = z.shape[0]
    c = w0.shape[-1]
    n_head = r_k.shape[-1]
    w2 = 2 * n_head
    assert w2 == LANES and (3 * c) % (2 * LANES) == 0 and LORA_W + LORA_A == LANES
    n_pairs = c // w2
    tc = min(tc, s)
    pairs_blk = min(pairs_blk, n_pairs)
    n_tb = s // tc
    ch = RWKV_CHUNK
    n_chunks = s // ch
    cpb = tc // ch
    lw = LORA_W + LORA_A + LORA_G
    cb = c // w2
    lb = 3 * c // lw
    mu2 = mu.reshape(1, -1)
    row = lambda v_: v_.reshape(1, c)

    def zspec(col_blk_fn, width):
        return pl.BlockSpec((tc, width), lambda bi, p, ti: (bi * n_tb + ti, col_blk_fn(p)))

    def hspec(col_blk_fn, width):
        return pl.BlockSpec((8, width),
                            lambda bi, p, ti: (jnp.maximum((bi * n_tb + ti) * (tc // 8) - 1, 0), col_blk_fn(p)))

    def pspec(col_blk_fn, width, rows=1):
        return pl.BlockSpec((rows, width), lambda bi, p, ti: (0, col_blk_fn(p)))

    sec = [lambda p: p, lambda p: cb + p, lambda p: 2 * cb + p]
    in_specs = ([zspec(f, w2) for f in sec] + [zspec(lambda p: lb, lw)]
                + [hspec(f, w2) for f in sec] + [hspec(lambda p: lb, lw)]
                + [pspec(f, w2) for f in sec] + [pspec(lambda p: lb, lw)]
                + [pspec(sec[0], w2) for _ in range(5)]
                + [pspec(sec[0], w2, LORA_W), pspec(sec[0], w2, LORA_A), pspec(sec[0], w2, LORA_G)])
    mat_shape = jax.ShapeDtypeStruct((b * n_pairs, n_chunks, w2, w2), F32)
    slab_shape = jax.ShapeDtypeStruct((n_pairs, t, w2), F32)
    mat_spec = pl.BlockSpec((None, cpb, w2, w2), lambda bi, p, ti: (bi * n_pairs + p, ti, 0, 0))
    slab_spec = pl.BlockSpec((None, tc, w2), lambda bi, p, ti: (p, bi * n_tb + ti, 0))
    m_mat, g_mat, rp, y0, bonus, gate = pl.pallas_call(
        functools.partial(_rwkv_chunk_kernel, tc=tc, n_head=n_head),
        out_shape=[mat_shape, mat_shape, slab_shape, slab_shape, slab_shape, slab_shape],
        grid=(b, n_pairs, n_tb),
        in_specs=in_specs,
        out_specs=[mat_spec, mat_spec, slab_spec, slab_spec, slab_spec, slab_spec],
        scratch_shapes=[pltpu.VMEM((tc, w2), F32)] * 3 + [pltpu.VMEM((tc, lw), F32)],
        compiler_params=_params(("parallel", "parallel", "arbitrary")),
        name="rwkv_chunk",
    )(z, z, z, z, z, z, z, z, mu2, mu2, mu2, mu2, row(w0), row(a0), row(k_k), row(k_a), row(r_k),
      w_up, a_up, g_up)

    n_pb = n_pairs // pairs_blk
    wblk = pairs_blk * w2
    mat_in = pl.BlockSpec((pairs_blk, 1, w2, w2), lambda bi, pb, ci: (bi * n_pb + pb, ci, 0, 0))
    slab_in = pl.BlockSpec((pairs_blk, ch, w2), lambda bi, pb, ci: (pb, bi * n_chunks + ci, 0))
    par_in = pl.BlockSpec((pairs_blk, 1, w2), lambda bi, pb, ci: (pb, 0, 0))
    by_pair = lambda v_: v_.reshape(n_pairs, 1, w2)
    return pl.pallas_call(
        functools.partial(_rwkv_state_kernel, n_pairs_blk=pairs_blk, n_head=n_head),
        out_shape=jax.ShapeDtypeStruct((t, c), BF16),
        grid=(b, n_pb, n_chunks),
        in_specs=[mat_in, mat_in, slab_in, slab_in, slab_in, slab_in, par_in, par_in],
        out_specs=pl.BlockSpec((ch, wblk), lambda bi, pb, ci: (bi * n_chunks + ci, pb)),
        scratch_shapes=[pltpu.VMEM((pairs_blk, w2, w2), F32)],
        compiler_params=_params(("parallel", "parallel", "arbitrary")),
        name="rwkv_state",
    )(m_mat, g_mat, rp, y0, bonus, gate, by_pair(ln_w), by_pair(ln_b))


def _router_kernel(x_ref, g_ref, w_ref, b_ref, h_ref, logit_ref, *, eps):
    x = x_ref[...]
    ms = jnp.mean(x * x, axis=-1, keepdims=True)
    h = x * lax.rsqrt(ms + eps) * g_ref[...]
    h_ref[...] = h
    logit_ref[...] = _dot_x3(h, w_ref[...]) + b_ref[...]


def _router(x, g, w_router, b_router, tm=256):
    t, d = x.shape
    n = w_router.shape[1]
    return pl.pallas_call(
        functools.partial(_router_kernel, eps=RMS_EPS),
        out_shape=[jax.ShapeDtypeStruct((t, d), F32), jax.ShapeDtypeStruct((t, n), F32)],
        grid=(t // tm,),
        in_specs=[pl.BlockSpec((tm, d), lambda i: (i, 0)), pl.BlockSpec((1, d), lambda i: (0, 0)),
                  pl.BlockSpec((d, n), lambda i: (0, 0)), pl.BlockSpec((1, n), lambda i: (0, 0))],
        out_specs=[pl.BlockSpec((tm, d), lambda i: (i, 0)), pl.BlockSpec((tm, n), lambda i: (i, 0))],
        compiler_params=_params(("parallel",)),
        name="moe_router",
    )(x, g.reshape(1, d), w_router, b_router)


def _gather_rows(idx_ref, base, src_hbm, dst_vmem, sem, n_rows):
    def issue(g, _):
        for u in range(DMA_ISSUE_UNROLL):
            r = g * DMA_ISSUE_UNROLL + u
            pltpu.make_async_copy(src_hbm.at[pl.ds(idx_ref[base + r], 1), :], dst_vmem.at[pl.ds(r, 1), :],
                                  sem).start(priority=u % 2)
        return 0
    lax.fori_loop(0, n_rows // DMA_ISSUE_UNROLL, issue, 0)


def _wait_rows(src_hbm, dst_vmem, sem, n_rows):
    pltpu.make_async_copy(src_hbm.at[pl.ds(0, n_rows), :], dst_vmem, sem).wait()


def _dispatch_kernel(tok_ref, n_used_ref, h_hbm, o_ref, buf, sem, *, blk):
    used = pl.program_id(0) < n_used_ref[0]

    @pl.when(used)
    def _():
        base = pl.program_id(0) * blk
        _gather_rows(tok_ref, base, h_hbm, buf, sem, blk)
        _wait_rows(h_hbm, buf, sem, blk)
        o_ref[...] = buf[...].astype(o_ref.dtype)

    @pl.when(jnp.logical_not(used))
    def _():
        o_ref[...] = jnp.zeros_like(o_ref)


def _dispatch(slot_tok, n_used, h, blk):
    n_slots = slot_tok.shape[0]
    d = h.shape[1]
    return pl.pallas_call(
        functools.partial(_dispatch_kernel, blk=blk),
        out_shape=jax.ShapeDtypeStruct((n_slots, d), BF16),
        grid_spec=pltpu.PrefetchScalarGridSpec(
            num_scalar_prefetch=2,
            grid=(n_slots // blk,),
            in_specs=[pl.BlockSpec(memory_space=pl.ANY)],
            out_specs=pl.BlockSpec((blk, d), lambda i, tok, nu: (i, 0)),
            scratch_shapes=[pltpu.VMEM((blk, d), F32), pltpu.SemaphoreType.DMA(())],
        ),
        compiler_params=_params(("arbitrary",)),
        name="moe_dispatch",
    )(slot_tok, n_used, h)


def _expert_blocks(off_ref, e, n_e, n_blk, blk, in_hbm, out_hbm, in_buf, out_buf, sem_in, sem_out, col, compute):
    b0 = off_ref[e]
    nb = off_ref[e + 1] - b0

    def rows(b):
        return pl.ds(pl.multiple_of(b * blk, blk), blk)

    def in_copy(b, slot):
        return pltpu.make_async_copy(in_hbm.at[rows(b0 + b), :], in_buf.at[slot], sem_in.at[slot])

    def out_copy(b, slot):
        return pltpu.make_async_copy(out_buf.at[slot], out_hbm.at[rows(b), col], sem_out.at[slot])

    @pl.when(nb > 0)
    def _():
        in_copy(0, 0).start()

    def body(b, _):
        slot = b & 1

        @pl.when(b + 1 < nb)
        def _():
            in_copy(b + 1, 1 - slot).start()

        in_copy(b, slot).wait()

        @pl.when(b >= 2)
        def _():
            out_copy(b0 + b - 2, slot).wait()

        out_buf[slot] = compute(in_buf[slot]).astype(out_buf.dtype)
        out_copy(b0 + b, slot).start()
        return 0

    lax.fori_loop(0, nb, body, 0)

    @pl.when(nb >= 2)
    def _():
        out_copy(b0 + nb - 2, nb & 1).wait()

    @pl.when(nb >= 1)
    def _():
        out_copy(b0 + nb - 1, (nb - 1) & 1).wait()

    @pl.when(e == n_e - 1)
    def _():
        out_buf[0] = jnp.zeros(out_buf.shape[1:], out_buf.dtype)
        lax.fori_loop(off_ref[n_e], n_blk, lambda b, _: (out_copy(b, 0).start(), 0)[1], 0)
        lax.fori_loop(off_ref[n_e], n_blk, lambda b, _: (out_copy(b, 0).wait(), 0)[1], 0)


def _expert_up_kernel(off_ref, x_hbm, wg_ref, wu_ref, o_hbm, x_buf, o_buf, wg_b, wu_b, sem_in, sem_out,
                      *, blk, tn, n_e, n_blk):
    wg_b[...] = wg_ref[...].astype(BF16)
    wu_b[...] = wu_ref[...].astype(BF16)

    def compute(x):
        gate = _dot(x, wg_b[...])
        up = _dot(x, wu_b[...])
        return gate * _sigmoid(gate) * up

    col = pl.ds(pl.multiple_of(pl.program_id(0) * tn, tn), tn)
    _expert_blocks(off_ref, pl.program_id(1), n_e, n_blk, blk, x_hbm, o_hbm, x_buf, o_buf, sem_in, sem_out,
                   col, compute)


def _expert_down_kernel(off_ref, h_hbm, wd_ref, o_hbm, h_buf, o_buf, wd_b, sem_in, sem_out, *, blk, n_e, n_blk):
    wd_b[...] = wd_ref[...].astype(BF16)
    _expert_blocks(off_ref, pl.program_id(0), n_e, n_blk, blk, h_hbm, o_hbm, h_buf, o_buf, sem_in, sem_out,
                   slice(None), lambda h: _dot(h, wd_b[...]))


def _experts(blk_off, xs, wg, wu, wd, layer, blk, tn=256):
    n_slots, d = xs.shape
    n_e, de = wg.shape[1], wg.shape[-1]
    n_blk = n_slots // blk
    tn = min(tn, de)
    w_up_spec = pl.BlockSpec((None, None, d, tn), lambda j, e, off: (layer, e, 0, j))
    any_spec = pl.BlockSpec(memory_space=pl.ANY)
    dma2 = pltpu.SemaphoreType.DMA((2,))
    hid = pl.pallas_call(
        functools.partial(_expert_up_kernel, blk=blk, tn=tn, n_e=n_e, n_blk=n_blk),
        out_shape=jax.ShapeDtypeStruct((n_slots, de), BF16),
        grid_spec=pltpu.PrefetchScalarGridSpec(
            num_scalar_prefetch=1,
            grid=(de // tn, n_e),
            in_specs=[any_spec, w_up_spec, w_up_spec],
            out_specs=any_spec,
            scratch_shapes=[pltpu.VMEM((2, blk, d), BF16), pltpu.VMEM((2, blk, tn), BF16),
                            pltpu.VMEM((d, tn), BF16), pltpu.VMEM((d, tn), BF16), dma2, dma2],
        ),
        compiler_params=_params(("arbitrary", "arbitrary")),
        name="moe_expert_up",
    )(blk_off, xs, wg, wu)
    return pl.pallas_call(
        functools.partial(_expert_down_kernel, blk=blk, n_e=n_e, n_blk=n_blk),
        out_shape=jax.ShapeDtypeStruct((n_slots, d), F32),
        grid_spec=pltpu.PrefetchScalarGridSpec(
            num_scalar_prefetch=1,
            grid=(n_e,),
            in_specs=[any_spec, pl.BlockSpec((None, None, de, d), lambda e, off: (layer, e, 0, 0))],
            out_specs=any_spec,
            scratch_shapes=[pltpu.VMEM((2, blk, de), BF16), pltpu.VMEM((2, blk, d), F32),
                            pltpu.VMEM((de, d), BF16), dma2, dma2],
        ),
        compiler_params=_params(("arbitrary",)),
        name="moe_expert_down",
    )(blk_off, hid, wd)


def _combine_kernel(pos0_ref, pos1_ref, x_ref, w_ref, y_hbm, o_ref, buf0, buf1, sem0, sem1, *, tb):
    base = pl.program_id(0) * tb
    _gather_rows(pos0_ref, base, y_hbm, buf0, sem0, tb)
    _gather_rows(pos1_ref, base, y_hbm, buf1, sem1, tb)
    w = w_ref[...]
    _wait_rows(y_hbm, buf0, sem0, tb)
    _wait_rows(y_hbm, buf1, sem1, tb)
    o_ref[...] = x_ref[...] + (buf0[...] * w[:, 0:1] + buf1[...] * w[:, 1:2])


def _combine(pos0, pos1, x, wts, y, tb=256):
    t, d = x.shape
    return pl.pallas_call(
        functools.partial(_combine_kernel, tb=tb),
        out_shape=jax.ShapeDtypeStruct((t, d), F32),
        grid_spec=pltpu.PrefetchScalarGridSpec(
            num_scalar_prefetch=2,
            grid=(t // tb,),
            in_specs=[pl.BlockSpec((tb, d), lambda i, p0, p1: (i, 0)),
                      pl.BlockSpec((tb, TOP_K), lambda i, p0, p1: (i, 0)),
                      pl.BlockSpec(memory_space=pl.ANY)],
            out_specs=pl.BlockSpec((tb, d), lambda i, p0, p1: (i, 0)),
            scratch_shapes=[pltpu.VMEM((tb, d), F32), pltpu.VMEM((tb, d), F32),
                            pltpu.SemaphoreType.DMA(()), pltpu.SemaphoreType.DMA(())],
        ),
        compiler_params=_params(("arbitrary",)),
        name="moe_combine",
    )(pos0, pos1, x, wts, y)


def _route(logits, n_groups, n_experts, blk):
    t = logits.shape[0]
    epg = n_experts // n_groups
    g_logits = logits[:, :n_groups]
    p_group = jax.nn.softmax(g_logits, axis=-1)
    g_idx = jnp.argmax(g_logits, axis=-1)
    p_sel = jnp.take_along_axis(p_group, g_idx[:, None], axis=1)
    e_logits = logits[:, n_groups:n_groups + n_experts].reshape(t, n_groups, epg)
    e_sel = jnp.take_along_axis(e_logits, g_idx[:, None, None], axis=1)[:, 0]
    top_v, top_i = lax.top_k(e_sel, TOP_K)
    wts = jax.nn.softmax(top_v, axis=-1) * p_sel
    eid = (g_idx[:, None] * epg + top_i).astype(jnp.int32)
    e_flat = eid.reshape(-1)
    n_assign = e_flat.shape[0]
    onehot = (e_flat[:, None] == jnp.arange(n_experts, dtype=jnp.int32)[None, :]).astype(jnp.int32)
    csum = jnp.cumsum(onehot, axis=0)
    counts = csum[-1]
    rank = jnp.take_along_axis(csum, e_flat[:, None], axis=1)[:, 0] - 1
    padded = (counts + blk - 1) // blk * blk
    pends = jnp.cumsum(padded)
    pstarts = pends - padded
    pos = (pstarts[e_flat] + rank).astype(jnp.int32)
    n_blk = (n_assign + blk - 1) // blk + n_experts
    tok_flat = jnp.repeat(jnp.arange(t, dtype=jnp.int32), TOP_K)
    slot_tok = jnp.zeros((n_blk * blk,), jnp.int32).at[pos].set(tok_flat)
    blk_off = (jnp.concatenate([pstarts, pends[-1:]]) // blk).astype(jnp.int32)
    pos2 = pos.reshape(t, TOP_K)
    return slot_tok, blk_off, pos2[:, 0], pos2[:, 1], wts.astype(F32)


def _hier_moe(x, ffn_norm, w_rg, b_rg, w_re, b_re, wg, wu, wd, layer):
    n_groups, n_experts = w_rg.shape[1], w_re.shape[1]
    n_r = n_groups + n_experts
    n_pad = -n_r % LANES
    w_router = jnp.pad(jnp.concatenate([w_rg, w_re], axis=1), ((0, 0), (0, n_pad)))
    b_router = jnp.pad(jnp.concatenate([b_rg, b_re], axis=0), (0, n_pad)).reshape(1, -1)
    h, logits = _router(x, ffn_norm, w_router, b_router)
    slot_tok, blk_off, pos0, pos1, wts = _route(logits, n_groups, n_experts, MOE_BLK)
    xs = _dispatch(slot_tok, blk_off[-1:], h, MOE_BLK)
    y = _experts(blk_off, xs, wg, wu, wd, layer, MOE_BLK)
    return _combine(pos0, pos1, x, wts, y)


def kernel(x, attn_norm, w_in, diff_lambda, diff_head_norm, rwkv_mu, rwkv_w0, rwkv_w_up, rwkv_a0, rwkv_a_up, rwkv_g_up, rwkv_k_k, rwkv_k_a, rwkv_r_k, rwkv_ln_w, rwkv_ln_b, w_out, ffn_norm, router_group, router_group_bias, router_expert, router_expert_bias, expert_w_gate, expert_w_up, expert_w_down, final_norm):
    b, s, d = x.shape
    depth = w_in.shape[0]
    t = b * s
    dv = diff_head_norm.shape[-1]
    rw = rwkv_w0.shape[-1]
    dw = d - rw
    n_heads = dw // dv
    attn_cols = 3 * dw
    rwkv_cols = w_in.shape[-1] - attn_cols
    slopes = jnp.asarray([LOG2E * 2.0 ** (-8.0 * (i + 1) / n_heads) for i in range(n_heads)], F32)
    tm = min(1024, t)

    xt = x.reshape(t, d)
    for l in range(depth):
        lam_init = 0.8 - 0.6 * math.exp(-0.3 * l)
        h = _rmsnorm(xt, attn_norm[l], BF16)
        proj_a = _matmul([h], w_in, l, 0, attn_cols, BF16, tm, _tile(attn_cols, (512, 256, 128)), name="in_proj_attn")
        proj_r = _matmul([h], w_in, l, attn_cols, rwkv_cols, F32, tm, 256, name="in_proj_rwkv")
        y_diff = _diff_attention(proj_a, diff_lambda[l], diff_head_norm[l], slopes, b, s, n_heads, lam_init)
        y_rwkv = _rwkv7(proj_r, rwkv_mu[l], rwkv_w0[l], rwkv_w_up[l], rwkv_a0[l], rwkv_a_up[l], rwkv_g_up[l],
                        rwkv_k_k[l], rwkv_k_a[l], rwkv_r_k[l], rwkv_ln_w[l], rwkv_ln_b[l], b, s)
        xt = _matmul([y_diff, y_rwkv], w_out, l, 0, d, F32, tm, _tile(d, (512, 256, 128)), res=xt, name="out_proj")
        xt = _hier_moe(xt, ffn_norm[l], router_group[l], router_group_bias[l], router_expert[l],
                       router_expert_bias[l], expert_w_gate, expert_w_up, expert_w_down, l)
    return _rmsnorm(xt, final_norm, F32).reshape(b, s, d)
```

```python
import functools
import math

import jax
import jax.numpy as jnp
from jax import lax
from jax.experimental import pallas as pl
from jax.experimental.pallas import tpu as pltpu

F32 = jnp.float32
BF16 = jnp.bfloat16

LANES = 128
VMEM_BUDGET_BYTES = 56 << 20

MASK_CHUNK = 64
RMS_EPS = 1e-6
HEAD_NORM_EPS = 1e-5
GN_EPS = 64e-5
DECAY_SCALE = math.exp(-0.5)
TOP_K = 2
LORA_W = 64
LORA_A = 64
LORA_G = 128

ATTN_BLOCK = 512
LOG2E = math.log2(math.e)
RWKV_CHUNK = 64
RWKV_GROUP = 8
RWKV_BLOCK = 2048
MOE_BLK = 256
DMA_ISSUE_UNROLL = 8
BLOCK_DMA_PRIORITY = 1


def _tile(n, candidates):
    return next(c for c in candidates if n % c == 0)


def _params(semantics):
    return pltpu.CompilerParams(dimension_semantics=semantics, vmem_limit_bytes=VMEM_BUDGET_BYTES)


def _dot(a, b):
    return jnp.dot(a, b, preferred_element_type=F32)


def _dot_nt(a, b):
    return lax.dot_general(a, b, (((1,), (1,)), ((), ())), preferred_element_type=F32)


def _dot_tn(a, b):
    return lax.dot_general(a, b, (((0,), (0,)), ((), ())), preferred_element_type=F32)


def _split2(x):
    hi = x.astype(BF16)
    lo = (x - hi.astype(F32)).astype(BF16)
    return hi, lo


def _split3(x):
    hi = x.astype(BF16)
    r1 = x - hi.astype(F32)
    mid = r1.astype(BF16)
    lo = (r1 - mid.astype(F32)).astype(BF16)
    return hi, mid, lo


def _dot_x3(a, b):
    ah, al = _split2(a)
    bh, bl = _split2(b)
    return _dot(ah, bh) + _dot(ah, bl) + _dot(al, bh)


def _iota(shape, dim):
    return lax.broadcasted_iota(jnp.int32, shape, dim)


def _sigmoid(x):
    return 1.0 / (1.0 + jnp.exp(-x))


def _rmsnorm_kernel(x_ref, g_ref, o_ref, *, eps):
    x = x_ref[...]
    ms = jnp.mean(x * x, axis=-1, keepdims=True)
    o_ref[...] = (x * lax.rsqrt(ms + eps) * g_ref[...]).astype(o_ref.dtype)


def _rmsnorm(x, g, out_dtype, tm=256):
    t, d = x.shape
    return pl.pallas_call(
        functools.partial(_rmsnorm_kernel, eps=RMS_EPS),
        out_shape=jax.ShapeDtypeStruct((t, d), out_dtype),
        grid=(t // tm,),
        in_specs=[pl.BlockSpec((tm, d), lambda i: (i, 0)), pl.BlockSpec((1, d), lambda i: (0, 0))],
        out_specs=pl.BlockSpec((tm, d), lambda i: (i, 0)),
        compiler_params=_params(("parallel",)),
        name="rmsnorm",
    )(x, g.reshape(1, d))


def _mm_kernel(*refs, n_a, has_res):
    a_refs, w_refs = refs[:n_a], refs[n_a:2 * n_a]
    o_ref = refs[-1]
    acc = None
    for a_ref, w_ref in zip(a_refs, w_refs):
        d = _dot(a_ref[...], w_ref[...].astype(BF16))
        acc = d if acc is None else acc + d
    if has_res:
        acc = acc + refs[2 * n_a][...]
    o_ref[...] = acc.astype(o_ref.dtype)


def _matmul(a_list, w, layer, col_off, n_out, out_dtype, tm, tn, res=None, name="matmul"):
    m, ka = a_list[0].shape
    n_a = len(a_list)
    assert w.shape[1] == ka * n_a and col_off % tn == 0 and n_out % tn == 0 and m % tm == 0
    in_specs = [pl.BlockSpec((tm, ka), lambda i, j: (i, 0)) for _ in a_list]
    in_specs += [pl.BlockSpec((None, ka, tn),
                              functools.partial(lambda i, j, r, c: (layer, r, j + c), r=r, c=col_off // tn))
                 for r in range(n_a)]
    args = list(a_list) + [w] * n_a
    if res is not None:
        in_specs.append(pl.BlockSpec((tm, tn), lambda i, j: (i, j)))
        args.append(res)
    return pl.pallas_call(
        functools.partial(_mm_kernel, n_a=n_a, has_res=res is not None),
        out_shape=jax.ShapeDtypeStruct((m, n_out), out_dtype),
        grid=(m // tm, n_out // tn),
        in_specs=in_specs,
        out_specs=pl.BlockSpec((tm, tn), lambda i, j: (i, j)),
        compiler_params=_params(("parallel", "arbitrary")),
        name=name,
    )(*args)


def _attn_kernel(slopes_ref, lam_ref, diag_ref, q_ref, k_ref, v_ref, hn_ref, o_ref, *, tq, tk, dk, lam_init):
    h = pl.program_id(1)
    qi = pl.program_id(2)
    slope = slopes_ref[h]
    q0 = qi * tq
    dv = q_ref.shape[1]

    lam_p = lam_ref[...]
    lam = (jnp.exp(jnp.sum(lam_p[0:1] * lam_p[1:2], keepdims=True))
           - jnp.exp(jnp.sum(lam_p[2:3] * lam_p[3:4], keepdims=True)) + lam_init)

    q = q_ref[...] * (dk ** -0.5 * LOG2E)
    lane = _iota((tq, dv), 1)
    zero = jnp.zeros_like(q)
    q2 = jnp.concatenate([jnp.where(lane < dk, q, zero), jnp.where(lane >= dk, q, zero)], axis=0)

    def blk(i):
        return pl.multiple_of(i * tk, tk)

    def scores(k0):
        return _dot_nt(q2, k_ref[pl.ds(k0, tk), :])

    def update(s, k0, masked, carry):
        v = v_ref[pl.ds(k0, tk), :]
        m, l, acc = carry
        if masked:
            bias = slope * diag_ref[...]
            s = s + jnp.concatenate([bias, bias], axis=0)
        else:
            s = s + slope * (k0 - q0 + _iota((1, tk), 1)).astype(F32)
        m_new = jnp.maximum(m, jnp.max(s, axis=-1, keepdims=True))
        alpha = jnp.exp2(m - m_new)
        p = jnp.exp2(s - m_new)
        l = alpha * l + jnp.sum(p, axis=-1, keepdims=True)
        acc = alpha * acc + _dot(p.astype(BF16), v)
        return m_new, l, acc

    carry = (jnp.full((2 * tq, 1), -jnp.inf, F32), jnp.zeros((2 * tq, 1), F32), jnp.zeros((2 * tq, dv), F32))
    def pair(j, c):
        ka, kb = blk(2 * j), blk(2 * j + 1)
        sa, sb = scores(ka), scores(kb)
        return update(sb, kb, False, update(sa, ka, False, c))

    def tail_odd(c):
        ka, kb = blk(qi - 1), blk(qi)
        sa, sb = scores(ka), scores(kb)
        return update(sb, kb, True, update(sa, ka, False, c))

    def tail_even(c):
        return update(scores(blk(qi)), blk(qi), True, c)

    carry = lax.fori_loop(0, jnp.right_shift(qi, 1), pair, carry)
    carry = lax.cond((qi & 1) == 1, tail_odd, tail_even, carry)
    _, l, acc = carry
    o = acc / l
    o = o[:tq] - lam * o[tq:]
    ms = jnp.mean(o * o, axis=-1, keepdims=True)
    o = o * lax.rsqrt(ms + HEAD_NORM_EPS) * hn_ref[...]
    o_ref[...] = (o * (1.0 - lam_init)).astype(o_ref.dtype)


def _diff_attention(proj, lam_p, head_norm, slopes, b, s, n_heads, lam_init):
    t = proj.shape[0]
    dv = head_norm.shape[-1]
    dk = lam_p.shape[-1]
    tq = tk = min(ATTN_BLOCK, s)
    assert s % tq == 0 and tq % MASK_CHUNK == 0
    nq = s // tq
    r, c = jnp.arange(tq)[:, None], jnp.arange(tk)[None, :]
    diag = jnp.where(c // MASK_CHUNK <= r // MASK_CHUNK, (r - jnp.abs(r - c)).astype(F32), -jnp.inf)
    return pl.pallas_call(
        functools.partial(_attn_kernel, tq=tq, tk=tk, dk=dk, lam_init=lam_init),
        out_shape=jax.ShapeDtypeStruct((t, n_heads * dv), BF16),
        grid=(b, n_heads, nq),
        in_specs=[
            pl.BlockSpec(memory_space=pltpu.SMEM),
            pl.BlockSpec(lam_p.shape, lambda bi, h, qi: (0, 0)),
            pl.BlockSpec((tq, tk), lambda bi, h, qi: (0, 0)),
            pl.BlockSpec((tq, dv), lambda bi, h, qi: (bi * nq + qi, h)),
            pl.BlockSpec((s, dv), lambda bi, h, qi: (bi, n_heads + h)),
            pl.BlockSpec((s, dv), lambda bi, h, qi: (bi, 2 * n_heads + h)),
            pl.BlockSpec((1, dv), lambda bi, h, qi: (0, 0)),
        ],
        out_specs=pl.BlockSpec((tq, dv), lambda bi, h, qi: (bi * nq + qi, h)),
        compiler_params=_params(("parallel", "parallel", "arbitrary")),
        name="diff_attention",
    )(slopes, lam_p, diag, proj, proj, proj, head_norm.reshape(1, dv))


def _rwkv_chunk_kernel(zr_ref, zk_ref, zv_ref, zl_ref, hr_ref, hk_ref, hv_ref, hl_ref,
                       mur_ref, muk_ref, muv_ref, mul_ref, w0_ref, a0_ref, kk_ref, ka_ref, rk_ref,
                       wup_ref, aup_ref, gup_ref,
                       m_ref, g_ref, rp_ref, y0_ref, bonus_ref, gate_ref,
                       sr, sk, sv, sl, *, tc, n_head):
    c = RWKV_CHUNK
    n_chunks = tc // c
    first = pl.program_id(2) == 0
    w2 = 2 * n_head
    pt = 2 * c

    def shifted(z_ref, h_ref, mu_ref, out_ref):
        z = z_ref[...]
        prev_row = jnp.where(first, 0.0, h_ref[7:8, :])
        zp = pltpu.roll(z, 1, axis=0)
        zp = jnp.where(_iota(z.shape, 0) == 0, prev_row, zp)
        out_ref[...] = z + mu_ref[...] * (zp - z)

    shifted(zr_ref, hr_ref, mur_ref, sr)
    shifted(zk_ref, hk_ref, muk_ref, sk)
    shifted(zv_ref, hv_ref, muv_ref, sv)
    shifted(zl_ref, hl_ref, mul_ref, sl)

    grp = min(RWKV_GROUP, n_chunks)
    n = grp * c
    m0 = _iota((c, w2), 1) < n_head
    m0n = _iota((n, w2), 1) < n_head
    row_p, col_p = _iota((pt, pt), 0), _iota((pt, pt), 1)
    same_head = (row_p // c) == (col_p // c)
    strict = same_head & ((row_p % c) > (col_p % c))
    incl = same_head & ((row_p % c) >= (col_p % c))
    eye_p = (row_p == col_p).astype(F32)
    row_l, col_l = _iota((w2, w2), 0), _iota((w2, w2), 1)
    head_blk = (row_l // n_head) == (col_l // n_head)
    seg_ones = head_blk.astype(BF16)
    row_n, col_n = _iota((n, n), 0), _iota((n, n), 1)
    tri = (((row_n // c) == (col_n // c)) & (row_n >= col_n)).astype(BF16)
    w_cat = jnp.concatenate([wup_ref[...], aup_ref[...]], axis=0)
    g_up = gup_ref[...].astype(BF16)

    def seg_sum(x):
        hi, lo = _split2(x)
        return _dot(hi, seg_ones) + _dot(lo, seg_ones)

    def bmm(a, b):
        return lax.dot_general(a, b, (((2,), (1,)), ((0,), (0,))), preferred_element_type=F32)

    def bmm_nt(a, b):
        return lax.dot_general(a, b, (((2,), (2,)), ((0,), (0,))), preferred_element_type=F32)

    def bmm_tn(a, b):
        return lax.dot_general(a, b, (((1,), (1,)), ((0,), (0,))), preferred_element_type=F32)

    def by_chunk(x):
        return x.reshape(grp, c, x.shape[-1])

    def stack(x):
        z = jnp.zeros_like(x)
        return jnp.concatenate([jnp.where(m0, x, z), jnp.where(m0, z, x)], axis=1)

    def unstack(x):
        return x[:, :c] + x[:, c:]

    PREP, SOLVE = 0, 1

    def chunk_group(gi):
        rows = slice(gi * n, (gi + 1) * n)
        r, k, v, lora = sr[rows, :], sk[rows, :], sv[rows, :], sl[rows, :]
        d_wa = lora[:, :LORA_W + LORA_A]
        zero = jnp.zeros_like(d_wa)
        logit_w = w0_ref[...] + _dot_x3(jnp.where(m0n, jnp.tanh(d_wa), zero), w_cat)
        logit_a = a0_ref[...] + _dot_x3(jnp.where(m0n, zero, d_wa), w_cat)
        yield PREP
        logw = -DECAY_SCALE * _sigmoid(logit_w)
        a = _sigmoid(logit_a)
        kk = k * kk_ref[...]
        kap = kk / jnp.maximum(jnp.sqrt(seg_sum(kk * kk)), 1e-12)
        yield PREP
        kh = k * (1.0 + (a - 1.0) * ka_ref[...])
        bvec = kap * a
        lw_h, lw_m, lw_l = _split3(logw)
        cum = _dot(tri, lw_h) + _dot(tri, lw_m) + _dot(tri, lw_l)
        yield PREP
        cum3 = by_chunk(cum)
        cum_end = cum3[:, c - 1:c, :]
        inv_dec = jnp.exp(-cum)
        kap_t = stack(by_chunk(kap * jnp.exp(cum - logw)))
        yield PREP
        r_t = stack(by_chunk(r * jnp.exp(cum)))
        lhs = jnp.concatenate([kap_t, r_t], axis=1).astype(BF16)
        yield PREP
        rhs = jnp.concatenate([stack(by_chunk(bvec * inv_dec)), stack(by_chunk(kh * inv_dec))],
                              axis=1).astype(BF16)
        yield PREP
        v3 = by_chunk(v)
        v_s = stack(v3).astype(BF16)
        dec_out = jnp.exp(cum_end - cum3)
        b_out = (by_chunk(bvec) * dec_out).astype(BF16)
        k_out = (by_chunk(kh) * dec_out).astype(BF16)
        yield PREP
        gate_ref[rows, :] = _dot(_sigmoid(lora[:, LORA_W + LORA_A:]).astype(BF16), g_up)
        bonus_ref[rows, :] = seg_sum(r * kh * rk_ref[...]) * v
        yield PREP

        aa = bmm_nt(lhs, rhs)
        kv = bmm_tn(k_out, v3.astype(BF16))
        yield SOLVE
        a_kb = jnp.where(strict, aa[:, :pt, :pt], 0.0)
        a_kk = jnp.where(strict, aa[:, :pt, pt:], 0.0)
        a_rb = jnp.where(incl, aa[:, pt:, :pt], 0.0).astype(BF16)
        a_rk = jnp.where(incl, aa[:, pt:, pt:], 0.0).astype(BF16)

        x = a_kb
        tinv = eye_p - x
        for _ in range(int(math.log2(c)) - 1):
            xb = x.astype(BF16)
            x = bmm(xb, xb)
            yield SOLVE
            tinv = bmm(tinv.astype(BF16), (eye_p + x).astype(BF16))
            yield SOLVE

        akv = bmm(a_kk.astype(BF16), v_s)
        yield SOLVE
        pq = bmm(tinv.astype(BF16), jnp.concatenate([kap_t, akv], axis=2).astype(BF16))
        yield SOLVE
        arb_pq = bmm(a_rb, pq.astype(BF16))
        rp = r_t - arb_pq[:, :, :w2]
        yield SOLVE
        y0 = bmm(a_rk, v_s) - arb_pq[:, :, w2:]
        pq_s = unstack(pq).astype(BF16)
        yield SOLVE
        bt_pq = bmm_tn(b_out, pq_s)
        m_mat = eye_p[:w2, :w2] * jnp.exp(cum_end) - jnp.where(head_blk, bt_pq[:, :, :w2], 0.0)
        g_mat = jnp.where(head_blk, kv - bt_pq[:, :, w2:], 0.0)

        chunks = slice(gi * grp, (gi + 1) * grp)
        m_ref[chunks] = m_mat
        g_ref[chunks] = g_mat
        rp_ref[rows, :] = unstack(rp).reshape(n, w2)
        y0_ref[rows, :] = unstack(y0).reshape(n, w2)
        yield SOLVE

    def run_prep(gen, max_stages):
        for _ in range(max_stages):
            if next(gen) == SOLVE:
                return False
        return True

    groups = [chunk_group(gi) for gi in range(n_chunks // grp)]
    run_prep(groups[0], 1 << 30)
    for gi, gen in enumerate(groups):
        nxt = groups[gi + 1] if gi + 1 < len(groups) else None
        nxt_in_prep = nxt is not None
        for _ in gen:
            if nxt_in_prep:
                nxt_in_prep = run_prep(nxt, 1)
        if nxt_in_prep:
            run_prep(nxt, 1 << 30)


def _rwkv_state_kernel(m_ref, g_ref, rp_ref, y0_ref, bonus_ref, gate_ref, lnw_ref, lnb_ref, o_ref, st_ref,
                       *, n_pairs_blk, n_head):
    @pl.when(pl.program_id(2) == 0)
    def _():
        st_ref[...] = jnp.zeros_like(st_ref)

    w2 = 2 * n_head
    head_blk = ((_iota((w2, w2), 0) // n_head) == (_iota((w2, w2), 1) // n_head)).astype(BF16)

    def seg_mean(x):
        hi, lo = _split2(x)
        return (_dot(hi, head_blk) + _dot(lo, head_blk)) * (1.0 / n_head)

    def bmm(a, b):
        return lax.dot_general(a, b, (((2,), (1,)), ((0,), (0,))), preferred_element_type=F32)

    ch = rp_ref.shape[1]
    st_hi, st_lo = _split2(st_ref[...])
    rp = rp_ref[...].astype(BF16)
    y = bmm(rp, st_hi) + bmm(rp, st_lo) + y0_ref[...]
    m = m_ref[:, 0].astype(BF16)
    st_ref[...] = bmm(m, st_hi) + bmm(m, st_lo) + g_ref[:, 0]
    y = y.reshape(n_pairs_blk * ch, w2)
    mean = seg_mean(y)
    yc = y - mean
    var = seg_mean(yc * yc)
    yn = (yc * lax.rsqrt(var + GN_EPS)).reshape(n_pairs_blk, ch, w2)
    out = (yn * lnw_ref[...] + lnb_ref[...] + bonus_ref[...]) * gate_ref[...]
    for p in range(n_pairs_blk):
        o_ref[:, p * w2:(p + 1) * w2] = out[p].astype(o_ref.dtype)


def _rwkv7(z, mu, w0, w_up, a0, a_up, g_up, k_k, k_a, r_k, ln_w, ln_b, b, s, tc=RWKV_BLOCK, pairs_blk=16):
    t = z.shape[0]
    c = w0.shape[-1]
    n_head = r_k.shape[-1]
    w2 = 2 * n_head
    assert w2 == LANES and (3 * c) % (2 * LANES) == 0 and LORA_W + LORA_A == LANES
    n_pairs = c // w2
    tc = min(tc, s)
    pairs_blk = min(pairs_blk, n_pairs)
    n_tb = s // tc
    ch = RWKV_CHUNK
    n_chunks = s // ch
    cpb = tc // ch
    lw = LORA_W + LORA_A + LORA_G
    cb = c // w2
    lb = 3 * c // lw
    mu2 = mu.reshape(1, -1)
    row = lambda v_: v_.reshape(1, c)

    def zspec(col_blk_fn, width):
        return pl.BlockSpec((tc, width), lambda bi, p, ti: (bi * n_tb + ti, col_blk_fn(p)))

    def hspec(col_blk_fn, width):
        return pl.BlockSpec((8, width),
                            lambda bi, p, ti: (jnp.maximum((bi * n_tb + ti) * (tc // 8) - 1, 0), col_blk_fn(p)))

    def pspec(col_blk_fn, width, rows=1):
        return pl.BlockSpec((rows, width), lambda bi, p, ti: (0, col_blk_fn(p)))

    sec = [lambda p: p, lambda p: cb + p, lambda p: 2 * cb + p]
    in_specs = ([zspec(f, w2) for f in sec] + [zspec(lambda p: lb, lw)]
                + [hspec(f, w2) for f in sec] + [hspec(lambda p: lb, lw)]
                + [pspec(f, w2) for f in sec] + [pspec(lambda p: lb, lw)]
                + [pspec(sec[0], w2) for _ in range(5)]
                + [pspec(sec[0], w2, LORA_W), pspec(sec[0], w2, LORA_A), pspec(sec[0], w2, LORA_G)])
    mat_shape = jax.ShapeDtypeStruct((b * n_pairs, n_chunks, w2, w2), F32)
    slab_shape = jax.ShapeDtypeStruct((n_pairs, t, w2), F32)
    mat_spec = pl.BlockSpec((None, cpb, w2, w2), lambda bi, p, ti: (bi * n_pairs + p, ti, 0, 0))
    slab_spec = pl.BlockSpec((None, tc, w2), lambda bi, p, ti: (p, bi * n_tb + ti, 0))
    m_mat, g_mat, rp, y0, bonus, gate = pl.pallas_call(
        functools.partial(_rwkv_chunk_kernel, tc=tc, n_head=n_head),
        out_shape=[mat_shape, mat_shape, slab_shape, slab_shape, slab_shape, slab_shape],
        grid=(b, n_pairs, n_tb),
        in_specs=in_specs,
        out_specs=[mat_spec, mat_spec, slab_spec, slab_spec, slab_spec, slab_spec],
        scratch_shapes=[pltpu.VMEM((tc, w2), F32)] * 3 + [pltpu.VMEM((tc, lw), F32)],
        compiler_params=_params(("parallel", "parallel", "arbitrary")),
        name="rwkv_chunk",
    )(z, z, z, z, z, z, z, z, mu2, mu2, mu2, mu2, row(w0), row(a0), row(k_k), row(k_a), row(r_k),
      w_up, a_up, g_up)

    n_pb = n_pairs // pairs_blk
    wblk = pairs_blk * w2
    mat_in = pl.BlockSpec((pairs_blk, 1, w2, w2), lambda bi, pb, ci: (bi * n_pb + pb, ci, 0, 0))
    slab_in = pl.BlockSpec((pairs_blk, ch, w2), lambda bi, pb, ci: (pb, bi * n_chunks + ci, 0))
    par_in = pl.BlockSpec((pairs_blk, 1, w2), lambda bi, pb, ci: (pb, 0, 0))
    by_pair = lambda v_: v_.reshape(n_pairs, 1, w2)
    return pl.pallas_call(
        functools.partial(_rwkv_state_kernel, n_pairs_blk=pairs_blk, n_head=n_head),
        out_shape=jax.ShapeDtypeStruct((t, c), BF16),
        grid=(b, n_pb, n_chunks),
        in_specs=[mat_in, mat_in, slab_in, slab_in, slab_in, slab_in, par_in, par_in],
        out_specs=pl.BlockSpec((ch, wblk), lambda bi, pb, ci: (bi * n_chunks + ci, pb)),
        scratch_shapes=[pltpu.VMEM((pairs_blk, w2, w2), F32)],
        compiler_params=_params(("parallel", "parallel", "arbitrary")),
        name="rwkv_state",
    )(m_mat, g_mat, rp, y0, bonus, gate, by_pair(ln_w), by_pair(ln_b))


def _router_kernel(x_ref, g_ref, w_ref, b_ref, h_ref, logit_ref, *, eps):
    x = x_ref[...]
    ms = jnp.mean(x * x, axis=-1, keepdims=True)
    h = x * lax.rsqrt(ms + eps) * g_ref[...]
    h_ref[...] = h
    logit_ref[...] = _dot_x3(h, w_ref[...]) + b_ref[...]


def _router(x, g, w_router, b_router, tm=256):
    t, d = x.shape
    n = w_router.shape[1]
    return pl.pallas_call(
        functools.partial(_router_kernel, eps=RMS_EPS),
        out_shape=[jax.ShapeDtypeStruct((t, d), F32), jax.ShapeDtypeStruct((t, n), F32)],
        grid=(t // tm,),
        in_specs=[pl.BlockSpec((tm, d), lambda i: (i, 0)), pl.BlockSpec((1, d), lambda i: (0, 0)),
                  pl.BlockSpec((d, n), lambda i: (0, 0)), pl.BlockSpec((1, n), lambda i: (0, 0))],
        out_specs=[pl.BlockSpec((tm, d), lambda i: (i, 0)), pl.BlockSpec((tm, n), lambda i: (i, 0))],
        compiler_params=_params(("parallel",)),
        name="moe_router",
    )(x, g.reshape(1, d), w_router, b_router)


def _gather_rows(idx_ref, base, src_hbm, dst_vmem, sem, n_rows):
    def issue(r, _):
        pltpu.make_async_copy(src_hbm.at[pl.ds(idx_ref[base + r], 1), :], dst_vmem.at[pl.ds(r, 1), :], sem).start()
        return 0
    lax.fori_loop(0, n_rows, issue, 0, unroll=DMA_ISSUE_UNROLL)


def _wait_rows(src_hbm, dst_vmem, sem, n_rows):
    pltpu.make_async_copy(src_hbm.at[pl.ds(0, n_rows), :], dst_vmem, sem).wait()


def _dispatch_kernel(tok_ref, n_used_ref, h_hbm, o_ref, buf, sem, *, blk):
    used = pl.program_id(0) < n_used_ref[0]

    @pl.when(used)
    def _():
        base = pl.program_id(0) * blk
        _gather_rows(tok_ref, base, h_hbm, buf, sem, blk)
        _wait_rows(h_hbm, buf, sem, blk)
        o_ref[...] = buf[...].astype(o_ref.dtype)

    @pl.when(jnp.logical_not(used))
    def _():
        o_ref[...] = jnp.zeros_like(o_ref)


def _dispatch(slot_tok, n_used, h, blk):
    n_slots = slot_tok.shape[0]
    d = h.shape[1]
    return pl.pallas_call(
        functools.partial(_dispatch_kernel, blk=blk),
        out_shape=jax.ShapeDtypeStruct((n_slots, d), BF16),
        grid_spec=pltpu.PrefetchScalarGridSpec(
            num_scalar_prefetch=2,
            grid=(n_slots // blk,),
            in_specs=[pl.BlockSpec(memory_space=pl.ANY)],
            out_specs=pl.BlockSpec((blk, d), lambda i, tok, nu: (i, 0)),
            scratch_shapes=[pltpu.VMEM((blk, d), F32), pltpu.SemaphoreType.DMA(())],
        ),
        compiler_params=_params(("arbitrary",)),
        name="moe_dispatch",
    )(slot_tok, n_used, h)


def _expert_blocks(off_ref, e, n_e, n_blk, blk, in_hbm, out_hbm, in_buf, out_buf, sem_in, sem_out, col, compute):
    b0 = off_ref[e]
    nb = off_ref[e + 1] - b0

    def rows(b):
        return pl.ds(pl.multiple_of(b * blk, blk), blk)

    def in_copy(b, slot):
        return pltpu.make_async_copy(in_hbm.at[rows(b0 + b), :], in_buf.at[slot], sem_in.at[slot])

    def out_copy(b, slot):
        return pltpu.make_async_copy(out_buf.at[slot], out_hbm.at[rows(b), col], sem_out.at[slot])

    @pl.when(nb > 0)
    def _():
        in_copy(0, 0).start(priority=BLOCK_DMA_PRIORITY)

    def body(b, _):
        slot = b & 1

        @pl.when(b + 1 < nb)
        def _():
            in_copy(b + 1, 1 - slot).start(priority=BLOCK_DMA_PRIORITY)

        in_copy(b, slot).wait()

        @pl.when(b >= 2)
        def _():
            out_copy(b0 + b - 2, slot).wait()

        out_buf[slot] = compute(in_buf[slot]).astype(out_buf.dtype)
        out_copy(b0 + b, slot).start(priority=BLOCK_DMA_PRIORITY)
        return 0

    lax.fori_loop(0, nb, body, 0)

    @pl.when(nb >= 2)
    def _():
        out_copy(b0 + nb - 2, nb & 1).wait()

    @pl.when(nb >= 1)
    def _():
        out_copy(b0 + nb - 1, (nb - 1) & 1).wait()

    @pl.when(e == n_e - 1)
    def _():
        out_buf[0] = jnp.zeros(out_buf.shape[1:], out_buf.dtype)
        lax.fori_loop(off_ref[n_e], n_blk, lambda b, _: (out_copy(b, 0).start(), 0)[1], 0)
        lax.fori_loop(off_ref[n_e], n_blk, lambda b, _: (out_copy(b, 0).wait(), 0)[1], 0)


def _expert_up_kernel(off_ref, x_hbm, wg_ref, wu_ref, o_hbm, x_buf, o_buf, sem_in, sem_out,
                      *, blk, tn, n_e, n_blk):
    def compute(x):
        gate = _dot(x, wg_ref[...].astype(BF16))
        up = _dot(x, wu_ref[...].astype(BF16))
        return gate * _sigmoid(gate) * up

    col = pl.ds(pl.multiple_of(pl.program_id(0) * tn, tn), tn)
    _expert_blocks(off_ref, pl.program_id(1), n_e, n_blk, blk, x_hbm, o_hbm, x_buf, o_buf, sem_in, sem_out,
                   col, compute)


def _expert_down_kernel(off_ref, h_hbm, wd_ref, o_hbm, h_buf, o_buf, sem_in, sem_out, *, blk, n_e, n_blk):
    _expert_blocks(off_ref, pl.program_id(0), n_e, n_blk, blk, h_hbm, o_hbm, h_buf, o_buf, sem_in, sem_out,
                   slice(None), lambda h: _dot(h, wd_ref[...].astype(BF16)))


def _experts(blk_off, xs, wg, wu, wd, layer, blk, tn=256):
    n_slots, d = xs.shape
    n_e, de = wg.shape[1], wg.shape[-1]
    n_blk = n_slots // blk
    tn = min(tn, de)
    w_up_spec = pl.BlockSpec((None, None, d, tn), lambda j, e, off: (layer, e, 0, j))
    any_spec = pl.BlockSpec(memory_space=pl.ANY)
    dma2 = pltpu.SemaphoreType.DMA((2,))
    hid = pl.pallas_call(
        functools.partial(_expert_up_kernel, blk=blk, tn=tn, n_e=n_e, n_blk=n_blk),
        out_shape=jax.ShapeDtypeStruct((n_slots, de), BF16),
        grid_spec=pltpu.PrefetchScalarGridSpec(
            num_scalar_prefetch=1,
            grid=(de // tn, n_e),
            in_specs=[any_spec, w_up_spec, w_up_spec],
            out_specs=any_spec,
            scratch_shapes=[pltpu.VMEM((2, blk, d), BF16), pltpu.VMEM((2, blk, tn), BF16), dma2, dma2],
        ),
        compiler_params=_params(("arbitrary", "arbitrary")),
        name="moe_expert_up",
    )(blk_off, xs, wg, wu)
    return pl.pallas_call(
        functools.partial(_expert_down_kernel, blk=blk, n_e=n_e, n_blk=n_blk),
        out_shape=jax.ShapeDtypeStruct((n_slots, d), F32),
        grid_spec=pltpu.PrefetchScalarGridSpec(
            num_scalar_prefetch=1,
            grid=(n_e,),
            in_specs=[any_spec, pl.BlockSpec((None, None, de, d), lambda e, off: (layer, e, 0, 0))],
            out_specs=any_spec,
            scratch_shapes=[pltpu.VMEM((2, blk, de), BF16), pltpu.VMEM((2, blk, d), F32), dma2, dma2],
        ),
        compiler_params=_params(("arbitrary",)),
        name="moe_expert_down",
    )(blk_off, hid, wd)


def _combine_kernel(pos0_ref, pos1_ref, x_ref, w_ref, y_hbm, o_ref, buf0, buf1, sem0, sem1, *, tb):
    base = pl.program_id(0) * tb
    _gather_rows(pos0_ref, base, y_hbm, buf0, sem0, tb)
    _gather_rows(pos1_ref, base, y_hbm, buf1, sem1, tb)
    w = w_ref[...]
    _wait_rows(y_hbm, buf0, sem0, tb)
    _wait_rows(y_hbm, buf1, sem1, tb)
    o_ref[...] = x_ref[...] + (buf0[...] * w[:, 0:1] + buf1[...] * w[:, 1:2])


def _combine(pos0, pos1, x, wts, y, tb=256):
    t, d = x.shape
    return pl.pallas_call(
        functools.partial(_combine_kernel, tb=tb),
        out_shape=jax.ShapeDtypeStruct((t, d), F32),
        grid_spec=pltpu.PrefetchScalarGridSpec(
            num_scalar_prefetch=2,
            grid=(t // tb,),
            in_specs=[pl.BlockSpec((tb, d), lambda i, p0, p1: (i, 0)),
                      pl.BlockSpec((tb, TOP_K), lambda i, p0, p1: (i, 0)),
                      pl.BlockSpec(memory_space=pl.ANY)],
            out_specs=pl.BlockSpec((tb, d), lambda i, p0, p1: (i, 0)),
            scratch_shapes=[pltpu.VMEM((tb, d), F32), pltpu.VMEM((tb, d), F32),
                            pltpu.SemaphoreType.DMA(()), pltpu.SemaphoreType.DMA(())],
        ),
        compiler_params=_params(("arbitrary",)),
        name="moe_combine",
    )(pos0, pos1, x, wts, y)


def _route(logits, n_groups, n_experts, blk):
    t = logits.shape[0]
    epg = n_experts // n_groups
    g_logits = logits[:, :n_groups]
    p_group = jax.nn.softmax(g_logits, axis=-1)
    g_idx = jnp.argmax(g_logits, axis=-1)
    p_sel = jnp.take_along_axis(p_group, g_idx[:, None], axis=1)
    e_logits = logits[:, n_groups:n_groups + n_experts].reshape(t, n_groups, epg)
    e_sel = jnp.take_along_axis(e_logits, g_idx[:, None, None], axis=1)[:, 0]
    top_v, top_i = lax.top_k(e_sel, TOP_K)
    wts = jax.nn.softmax(top_v, axis=-1) * p_sel
    eid = (g_idx[:, None] * epg + top_i).astype(jnp.int32)
    e_flat = eid.reshape(-1)
    n_assign = e_flat.shape[0]
    onehot = (e_flat[:, None] == jnp.arange(n_experts, dtype=jnp.int32)[None, :]).astype(jnp.int32)
    csum = jnp.cumsum(onehot, axis=0)
    counts = csum[-1]
    rank = jnp.take_along_axis(csum, e_flat[:, None], axis=1)[:, 0] - 1
    padded = (counts + blk - 1) // blk * blk
    pends = jnp.cumsum(padded)
    pstarts = pends - padded
    pos = (pstarts[e_flat] + rank).astype(jnp.int32)
    n_blk = (n_assign + blk - 1) // blk + n_experts
    tok_flat = jnp.repeat(jnp.arange(t, dtype=jnp.int32), TOP_K)
    slot_tok = jnp.zeros((n_blk * blk,), jnp.int32).at[pos].set(tok_flat)
    blk_off = (jnp.concatenate([pstarts, pends[-1:]]) // blk).astype(jnp.int32)
    pos2 = pos.reshape(t, TOP_K)
    return slot_tok, blk_off, pos2[:, 0], pos2[:, 1], wts.astype(F32)


def _hier_moe(x, ffn_norm, w_rg, b_rg, w_re, b_re, wg, wu, wd, layer):
    n_groups, n_experts = w_rg.shape[1], w_re.shape[1]
    n_r = n_groups + n_experts
    n_pad = -n_r % LANES
    w_router = jnp.pad(jnp.concatenate([w_rg, w_re], axis=1), ((0, 0), (0, n_pad)))
    b_router = jnp.pad(jnp.concatenate([b_rg, b_re], axis=0), (0, n_pad)).reshape(1, -1)
    h, logits = _router(x, ffn_norm, w_router, b_router)
    slot_tok, blk_off, pos0, pos1, wts = _route(logits, n_groups, n_experts, MOE_BLK)
    xs = _dispatch(slot_tok, blk_off[-1:], h, MOE_BLK)
    y = _experts(blk_off, xs, wg, wu, wd, layer, MOE_BLK)
    return _combine(pos0, pos1, x, wts, y)


def kernel(x, attn_norm, w_in, diff_lambda, diff_head_norm, rwkv_mu, rwkv_w0, rwkv_w_up, rwkv_a0, rwkv_a_up, rwkv_g_up, rwkv_k_k, rwkv_k_a, rwkv_r_k, rwkv_ln_w, rwkv_ln_b, w_out, ffn_norm, router_group, router_group_bias, router_expert, router_expert_bias, expert_w_gate, expert_w_up, expert_w_down, final_norm):
    b, s, d = x.shape
    depth = w_in.shape[0]
    t = b * s
    dv = diff_head_norm.shape[-1]
    rw = rwkv_w0.shape[-1]
    dw = d - rw
    n_heads = dw // dv
    attn_cols = 3 * dw
    rwkv_cols = w_in.shape[-1] - attn_cols
    slopes = jnp.asarray([LOG2E * 2.0 ** (-8.0 * (i + 1) / n_heads) for i in range(n_heads)], F32)
    tm = min(1024, t)

    xt = x.reshape(t, d)
    for l in range(depth):
        lam_init = 0.8 - 0.6 * math.exp(-0.3 * l)
        h = _rmsnorm(xt, attn_norm[l], BF16)
        proj_a = _matmul([h], w_in, l, 0, attn_cols, BF16, tm, _tile(attn_cols, (512, 256, 128)), name="in_proj_attn")
        proj_r = _matmul([h], w_in, l, attn_cols, rwkv_cols, F32, tm, 256, name="in_proj_rwkv")
        y_diff = _diff_attention(proj_a, diff_lambda[l], diff_head_norm[l], slopes, b, s, n_heads, lam_init)
        y_rwkv = _rwkv7(proj_r, rwkv_mu[l], rwkv_w0[l], rwkv_w_up[l], rwkv_a0[l], rwkv_a_up[l], rwkv_g_up[l],
                        rwkv_k_k[l], rwkv_k_a[l], rwkv_r_k[l], rwkv_ln_w[l], rwkv_ln_b[l], b, s)
        xt = _matmul([y_diff, y_rwkv], w_out, l, 0, d, F32, tm, _tile(d, (512, 256, 128)), res=xt, name="out_proj")
        xt = _hier_moe(xt, ffn_norm[l], router_group[l], router_group_bias[l], router_expert[l],
                       router_expert_bias[l], expert_w_gate, expert_w_up, expert_w_down, l)
    return _rmsnorm(xt, final_norm, F32).reshape(b, s, d)
```

```python
import functools
import math

import jax
import jax.numpy as jnp
from jax import lax
from jax.experimental import pallas as pl
from jax.experimental.pallas import tpu as pltpu

F32 = jnp.float32
BF16 = jnp.bfloat16

LANES = 128
VMEM_BUDGET_BYTES = 56 << 20

MASK_CHUNK = 64
RMS_EPS = 1e-6
HEAD_NORM_EPS = 1e-5
GN_EPS = 64e-5
DECAY_SCALE = math.exp(-0.5)
TOP_K = 2
LORA_W = 64
LORA_A = 64
LORA_G = 128

ATTN_BLOCK = 512
LOG2E = math.log2(math.e)
RWKV_CHUNK = 64
RWKV_GROUP = 8
RWKV_BLOCK = 2048
MOE_BLK = 256
DMA_ISSUE_UNROLL = 8


def _tile(n, candidates):
    return next(c for c in candidates if n % c == 0)


def _params(semantics):
    return pltpu.CompilerParams(dimension_semantics=semantics, vmem_limit_bytes=VMEM_BUDGET_BYTES)


def _dot(a, b):
    return jnp.dot(a, b, preferred_element_type=F32)


def _dot_nt(a, b):
    return lax.dot_general(a, b, (((1,), (1,)), ((), ())), preferred_element_type=F32)


def _dot_tn(a, b):
    return lax.dot_general(a, b, (((0,), (0,)), ((), ())), preferred_element_type=F32)


def _split2(x):
    hi = x.astype(BF16)
    lo = (x - hi.astype(F32)).astype(BF16)
    return hi, lo


def _split3(x):
    hi = x.astype(BF16)
    r1 = x - hi.astype(F32)
    mid = r1.astype(BF16)
    lo = (r1 - mid.astype(F32)).astype(BF16)
    return hi, mid, lo


def _dot_x3(a, b):
    ah, al = _split2(a)
    bh, bl = _split2(b)
    return _dot(ah, bh) + _dot(ah, bl) + _dot(al, bh)


def _iota(shape, dim):
    return lax.broadcasted_iota(jnp.int32, shape, dim)


def _sigmoid(x):
    return 1.0 / (1.0 + jnp.exp(-x))


def _rmsnorm_kernel(x_ref, g_ref, o_ref, *, eps):
    x = x_ref[...]
    ms = jnp.mean(x * x, axis=-1, keepdims=True)
    o_ref[...] = (x * lax.rsqrt(ms + eps) * g_ref[...]).astype(o_ref.dtype)


def _rmsnorm(x, g, out_dtype, tm=256):
    t, d = x.shape
    return pl.pallas_call(
        functools.partial(_rmsnorm_kernel, eps=RMS_EPS),
        out_shape=jax.ShapeDtypeStruct((t, d), out_dtype),
        grid=(t // tm,),
        in_specs=[pl.BlockSpec((tm, d), lambda i: (i, 0)), pl.BlockSpec((1, d), lambda i: (0, 0))],
        out_specs=pl.BlockSpec((tm, d), lambda i: (i, 0)),
        compiler_params=_params(("parallel",)),
        name="rmsnorm",
    )(x, g.reshape(1, d))


def _mm_kernel(*refs, n_a, has_res):
    a_refs, w_refs = refs[:n_a], refs[n_a:2 * n_a]
    o_ref = refs[-1]
    acc = None
    for a_ref, w_ref in zip(a_refs, w_refs):
        d = _dot(a_ref[...], w_ref[...].astype(BF16))
        acc = d if acc is None else acc + d
    if has_res:
        acc = acc + refs[2 * n_a][...]
    o_ref[...] = acc.astype(o_ref.dtype)


def _matmul(a_list, w, layer, col_off, n_out, out_dtype, tm, tn, res=None, name="matmul"):
    m, ka = a_list[0].shape
    n_a = len(a_list)
    assert w.shape[1] == ka * n_a and col_off % tn == 0 and n_out % tn == 0 and m % tm == 0
    in_specs = [pl.BlockSpec((tm, ka), lambda i, j: (i, 0)) for _ in a_list]
    in_specs += [pl.BlockSpec((None, ka, tn),
                              functools.partial(lambda i, j, r, c: (layer, r, j + c), r=r, c=col_off // tn))
                 for r in range(n_a)]
    args = list(a_list) + [w] * n_a
    if res is not None:
        in_specs.append(pl.BlockSpec((tm, tn), lambda i, j: (i, j)))
        args.append(res)
    return pl.pallas_call(
        functools.partial(_mm_kernel, n_a=n_a, has_res=res is not None),
        out_shape=jax.ShapeDtypeStruct((m, n_out), out_dtype),
        grid=(m // tm, n_out // tn),
        in_specs=in_specs,
        out_specs=pl.BlockSpec((tm, tn), lambda i, j: (i, j)),
        compiler_params=_params(("parallel", "arbitrary")),
        name=name,
    )(*args)


def _attn_kernel(slopes_ref, lam_ref, diag_ref, q_ref, k_ref, v_ref, hn_ref, o_ref, *, tq, tk, dk, lam_init):
    h = pl.program_id(1)
    qi = pl.program_id(2)
    slope = slopes_ref[h]
    q0 = qi * tq
    dv = q_ref.shape[1]

    lam_p = lam_ref[...]
    lam = (jnp.exp(jnp.sum(lam_p[0:1] * lam_p[1:2], keepdims=True))
           - jnp.exp(jnp.sum(lam_p[2:3] * lam_p[3:4], keepdims=True)) + lam_init)

    q = q_ref[...] * (dk ** -0.5 * LOG2E)
    lane = _iota((tq, dv), 1)
    zero = jnp.zeros_like(q)
    q2 = jnp.concatenate([jnp.where(lane < dk, q, zero), jnp.where(lane >= dk, q, zero)], axis=0)

    def blk(i):
        return pl.multiple_of(i * tk, tk)

    def scores(k0):
        return _dot_nt(q2, k_ref[pl.ds(k0, tk), :])

    def update(s, k0, masked, carry):
        v = v_ref[pl.ds(k0, tk), :]
        m, l, acc = carry
        if masked:
            bias = slope * diag_ref[...]
            s = s + jnp.concatenate([bias, bias], axis=0)
        else:
            s = s + slope * (k0 - q0 + _iota((1, tk), 1)).astype(F32)
        m_new = jnp.maximum(m, jnp.max(s, axis=-1, keepdims=True))
        alpha = jnp.exp2(m - m_new)
        p = jnp.exp2(s - m_new)
        l = alpha * l + jnp.sum(p, axis=-1, keepdims=True)
        acc = alpha * acc + _dot(p.astype(BF16), v)
        return m_new, l, acc

    carry = (jnp.full((2 * tq, 1), -jnp.inf, F32), jnp.zeros((2 * tq, 1), F32), jnp.zeros((2 * tq, dv), F32))
    def pair(j, c):
        ka, kb = blk(2 * j), blk(2 * j + 1)
        sa, sb = scores(ka), scores(kb)
        return update(sb, kb, False, update(sa, ka, False, c))

    def tail_odd(c):
        ka, kb = blk(qi - 1), blk(qi)
        sa, sb = scores(ka), scores(kb)
        return update(sb, kb, True, update(sa, ka, False, c))

    def tail_even(c):
        return update(scores(blk(qi)), blk(qi), True, c)

    carry = lax.fori_loop(0, jnp.right_shift(qi, 1), pair, carry)
    carry = lax.cond((qi & 1) == 1, tail_odd, tail_even, carry)
    _, l, acc = carry
    o = acc / l
    o = o[:tq] - lam * o[tq:]
    ms = jnp.mean(o * o, axis=-1, keepdims=True)
    o = o * lax.rsqrt(ms + HEAD_NORM_EPS) * hn_ref[...]
    o_ref[...] = (o * (1.0 - lam_init)).astype(o_ref.dtype)


def _diff_attention(proj, lam_p, head_norm, slopes, b, s, n_heads, lam_init):
    t = proj.shape[0]
    dv = head_norm.shape[-1]
    dk = lam_p.shape[-1]
    tq = tk = min(ATTN_BLOCK, s)
    assert s % tq == 0 and tq % MASK_CHUNK == 0
    nq = s // tq
    r, c = jnp.arange(tq)[:, None], jnp.arange(tk)[None, :]
    diag = jnp.where(c // MASK_CHUNK <= r // MASK_CHUNK, (r - jnp.abs(r - c)).astype(F32), -jnp.inf)
    return pl.pallas_call(
        functools.partial(_attn_kernel, tq=tq, tk=tk, dk=dk, lam_init=lam_init),
        out_shape=jax.ShapeDtypeStruct((t, n_heads * dv), BF16),
        grid=(b, n_heads, nq),
        in_specs=[
            pl.BlockSpec(memory_space=pltpu.SMEM),
            pl.BlockSpec(lam_p.shape, lambda bi, h, qi: (0, 0)),
            pl.BlockSpec((tq, tk), lambda bi, h, qi: (0, 0)),
            pl.BlockSpec((tq, dv), lambda bi, h, qi: (bi * nq + qi, h)),
            pl.BlockSpec((s, dv), lambda bi, h, qi: (bi, n_heads + h)),
            pl.BlockSpec((s, dv), lambda bi, h, qi: (bi, 2 * n_heads + h)),
            pl.BlockSpec((1, dv), lambda bi, h, qi: (0, 0)),
        ],
        out_specs=pl.BlockSpec((tq, dv), lambda bi, h, qi: (bi * nq + qi, h)),
        compiler_params=_params(("parallel", "parallel", "arbitrary")),
        name="diff_attention",
    )(slopes, lam_p, diag, proj, proj, proj, head_norm.reshape(1, dv))


def _rwkv_chunk_kernel(zr_ref, zk_ref, zv_ref, zl_ref, hr_ref, hk_ref, hv_ref, hl_ref,
                       mur_ref, muk_ref, muv_ref, mul_ref, w0_ref, a0_ref, kk_ref, ka_ref, rk_ref,
                       wup_ref, aup_ref, gup_ref,
                       m_ref, g_ref, rp_ref, y0_ref, bonus_ref, gate_ref,
                       sr, sk, sv, sl, *, tc, n_head):
    c = RWKV_CHUNK
    n_chunks = tc // c
    first = pl.program_id(2) == 0
    w2 = 2 * n_head
    pt = 2 * c

    def shifted(z_ref, h_ref, mu_ref, out_ref):
        z = z_ref[...]
        prev_row = jnp.where(first, 0.0, h_ref[7:8, :])
        zp = pltpu.roll(z, 1, axis=0)
        zp = jnp.where(_iota(z.shape, 0) == 0, prev_row, zp)
        out_ref[...] = z + mu_ref[...] * (zp - z)

    shifted(zr_ref, hr_ref, mur_ref, sr)
    shifted(zk_ref, hk_ref, muk_ref, sk)
    shifted(zv_ref, hv_ref, muv_ref, sv)
    shifted(zl_ref, hl_ref, mul_ref, sl)

    grp = min(RWKV_GROUP, n_chunks)
    n = grp * c
    m0 = _iota((c, w2), 1) < n_head
    m0n = _iota((n, w2), 1) < n_head
    row_p, col_p = _iota((pt, pt), 0), _iota((pt, pt), 1)
    same_head = (row_p // c) == (col_p // c)
    strict = same_head & ((row_p % c) > (col_p % c))
    incl = same_head & ((row_p % c) >= (col_p % c))
    eye_p = (row_p == col_p).astype(F32)
    row_l, col_l = _iota((w2, w2), 0), _iota((w2, w2), 1)
    head_blk = (row_l // n_head) == (col_l // n_head)
    seg_ones = head_blk.astype(BF16)
    row_n, col_n = _iota((n, n), 0), _iota((n, n), 1)
    tri = (((row_n // c) == (col_n // c)) & (row_n >= col_n)).astype(BF16)
    w_cat = jnp.concatenate([wup_ref[...], aup_ref[...]], axis=0)
    g_up = gup_ref[...].astype(BF16)

    def seg_sum(x):
        hi, lo = _split2(x)
        return _dot(hi, seg_ones) + _dot(lo, seg_ones)

    def bmm(a, b):
        return lax.dot_general(a, b, (((2,), (1,)), ((0,), (0,))), preferred_element_type=F32)

    def bmm_nt(a, b):
        return lax.dot_general(a, b, (((2,), (2,)), ((0,), (0,))), preferred_element_type=F32)

    def bmm_tn(a, b):
        return lax.dot_general(a, b, (((1,), (1,)), ((0,), (0,))), preferred_element_type=F32)

    def by_chunk(x):
        return x.reshape(grp, c, x.shape[-1])

    def stack(x):
        z = jnp.zeros_like(x)
        return jnp.concatenate([jnp.where(m0, x, z), jnp.where(m0, z, x)], axis=1)

    def unstack(x):
        return x[:, :c] + x[:, c:]

    PREP, SOLVE = 0, 1

    def chunk_group(gi):
        rows = slice(gi * n, (gi + 1) * n)
        r, k, v, lora = sr[rows, :], sk[rows, :], sv[rows, :], sl[rows, :]
        d_wa = lora[:, :LORA_W + LORA_A]
        zero = jnp.zeros_like(d_wa)
        logit_w = w0_ref[...] + _dot_x3(jnp.where(m0n, jnp.tanh(d_wa), zero), w_cat)
        logit_a = a0_ref[...] + _dot_x3(jnp.where(m0n, zero, d_wa), w_cat)
        yield PREP
        logw = -DECAY_SCALE * _sigmoid(logit_w)
        a = _sigmoid(logit_a)
        kk = k * kk_ref[...]
        kap = kk / jnp.maximum(jnp.sqrt(seg_sum(kk * kk)), 1e-12)
        yield PREP
        kh = k * (1.0 + (a - 1.0) * ka_ref[...])
        bvec = kap * a
        lw_h, lw_m, lw_l = _split3(logw)
        cum = _dot(tri, lw_h) + _dot(tri, lw_m) + _dot(tri, lw_l)
        yield PREP
        cum3 = by_chunk(cum)
        cum_end = cum3[:, c - 1:c, :]
        inv_dec = jnp.exp(-cum)
        kap_t = stack(by_chunk(kap * jnp.exp(cum - logw)))
        yield PREP
        r_t = stack(by_chunk(r * jnp.exp(cum)))
        lhs = jnp.concatenate([kap_t, r_t], axis=1).astype(BF16)
        yield PREP
        rhs = jnp.concatenate([stack(by_chunk(bvec * inv_dec)), stack(by_chunk(kh * inv_dec))],
                              axis=1).astype(BF16)
        yield PREP
        v3 = by_chunk(v)
        v_s = stack(v3).astype(BF16)
        dec_out = jnp.exp(cum_end - cum3)
        b_out = (by_chunk(bvec) * dec_out).astype(BF16)
        k_out = (by_chunk(kh) * dec_out).astype(BF16)
        yield PREP
        gate_ref[rows, :] = _dot(_sigmoid(lora[:, LORA_W + LORA_A:]).astype(BF16), g_up)
        bonus_ref[rows, :] = seg_sum(r * kh * rk_ref[...]) * v
        yield PREP

        aa = bmm_nt(lhs, rhs)
        kv = bmm_tn(k_out, v3.astype(BF16))
        yield SOLVE
        a_kb = jnp.where(strict, aa[:, :pt, :pt], 0.0)
        a_kk = jnp.where(strict, aa[:, :pt, pt:], 0.0)
        a_rb = jnp.where(incl, aa[:, pt:, :pt], 0.0).astype(BF16)
        a_rk = jnp.where(incl, aa[:, pt:, pt:], 0.0).astype(BF16)

        x = a_kb
        tinv = eye_p - x
        for _ in range(int(math.log2(c)) - 1):
            xb = x.astype(BF16)
            x = bmm(xb, xb)
            yield SOLVE
            tinv = bmm(tinv.astype(BF16), (eye_p + x).astype(BF16))
            yield SOLVE

        akv = bmm(a_kk.astype(BF16), v_s)
        yield SOLVE
        pq = bmm(tinv.astype(BF16), jnp.concatenate([kap_t, akv], axis=2).astype(BF16))
        yield SOLVE
        arb_pq = bmm(a_rb, pq.astype(BF16))
        rp = r_t - arb_pq[:, :, :w2]
        yield SOLVE
        y0 = bmm(a_rk, v_s) - arb_pq[:, :, w2:]
        pq_s = unstack(pq).astype(BF16)
        yield SOLVE
        bt_pq = bmm_tn(b_out, pq_s)
        m_mat = eye_p[:w2, :w2] * jnp.exp(cum_end) - jnp.where(head_blk, bt_pq[:, :, :w2], 0.0)
        g_mat = jnp.where(head_blk, kv - bt_pq[:, :, w2:], 0.0)

        chunks = slice(gi * grp, (gi + 1) * grp)
        m_ref[chunks] = m_mat
        g_ref[chunks] = g_mat
        rp_ref[rows, :] = unstack(rp).reshape(n, w2)
        y0_ref[rows, :] = unstack(y0).reshape(n, w2)
        yield SOLVE

    def run_prep(gen, max_stages):
        for _ in range(max_stages):
            if next(gen) == SOLVE:
                return False
        return True

    groups = [chunk_group(gi) for gi in range(n_chunks // grp)]
    run_prep(groups[0], 1 << 30)
    for gi, gen in enumerate(groups):
        nxt = groups[gi + 1] if gi + 1 < len(groups) else None
        nxt_in_prep = nxt is not None
        for _ in gen:
            if nxt_in_prep:
                nxt_in_prep = run_prep(nxt, 1)
        if nxt_in_prep:
            run_prep(nxt, 1 << 30)


def _rwkv_state_kernel(m_ref, g_ref, rp_ref, y0_ref, bonus_ref, gate_ref, lnw_ref, lnb_ref, o_ref, st_ref,
                       *, n_pairs_blk, n_head):
    @pl.when(pl.program_id(2) == 0)
    def _():
        st_ref[...] = jnp.zeros_like(st_ref)

    w2 = 2 * n_head
    head_blk = ((_iota((w2, w2), 0) // n_head) == (_iota((w2, w2), 1) // n_head)).astype(BF16)

    def seg_mean(x):
        hi, lo = _split2(x)
        return (_dot(hi, head_blk) + _dot(lo, head_blk)) * (1.0 / n_head)

    def bmm(a, b):
        return lax.dot_general(a, b, (((2,), (1,)), ((0,), (0,))), preferred_element_type=F32)

    ch = rp_ref.shape[1]
    st_hi, st_lo = _split2(st_ref[...])
    rp = rp_ref[...].astype(BF16)
    y = bmm(rp, st_hi) + bmm(rp, st_lo) + y0_ref[...]
    m = m_ref[:, 0].astype(BF16)
    st_ref[...] = bmm(m, st_hi) + bmm(m, st_lo) + g_ref[:, 0]
    y = y.reshape(n_pairs_blk * ch, w2)
    mean = seg_mean(y)
    yc = y - mean
    var = seg_mean(yc * yc)
    yn = (yc * lax.rsqrt(var + GN_EPS)).reshape(n_pairs_blk, ch, w2)
    out = (yn * lnw_ref[...] + lnb_ref[...] + bonus_ref[...]) * gate_ref[...]
    for p in range(n_pairs_blk):
        o_ref[:, p * w2:(p + 1) * w2] = out[p].astype(o_ref.dtype)


def _rwkv7(z, mu, w0, w_up, a0, a_up, g_up, k_k, k_a, r_k, ln_w, ln_b, b, s, tc=RWKV_BLOCK, pairs_blk=16):
    t = z.shape[0]
    c = w0.shape[-1]
    n_head = r_k.shape[-1]
    w2 = 2 * n_head
    assert w2 == LANES and (3 * c) % (2 * LANES) == 0 and LORA_W + LORA_A == LANES
    n_pairs = c // w2
    tc = min(tc, s)
    pairs_blk = min(pairs_blk, n_pairs)
    n_tb = s // tc
    ch = RWKV_CHUNK
    n_chunks = s // ch
    cpb = tc // ch
    lw = LORA_W + LORA_A + LORA_G
    cb = c // w2
    lb = 3 * c // lw
    mu2 = mu.reshape(1, -1)
    row = lambda v_: v_.reshape(1, c)

    def zspec(col_blk_fn, width):
        return pl.BlockSpec((tc, width), lambda bi, p, ti: (bi * n_tb + ti, col_blk_fn(p)))

    def hspec(col_blk_fn, width):
        return pl.BlockSpec((8, width),
                            lambda bi, p, ti: (jnp.maximum((bi * n_tb + ti) * (tc // 8) - 1, 0), col_blk_fn(p)))

    def pspec(col_blk_fn, width, rows=1):
        return pl.BlockSpec((rows, width), lambda bi, p, ti: (0, col_blk_fn(p)))

    sec = [lambda p: p, lambda p: cb + p, lambda p: 2 * cb + p]
    in_specs = ([zspec(f, w2) for f in sec] + [zspec(lambda p: lb, lw)]
                + [hspec(f, w2) for f in sec] + [hspec(lambda p: lb, lw)]
                + [pspec(f, w2) for f in sec] + [pspec(lambda p: lb, lw)]
                + [pspec(sec[0], w2) for _ in range(5)]
                + [pspec(sec[0], w2, LORA_W), pspec(sec[0], w2, LORA_A), pspec(sec[0], w2, LORA_G)])
    mat_shape = jax.ShapeDtypeStruct((b * n_pairs, n_chunks, w2, w2), F32)
    slab_shape = jax.ShapeDtypeStruct((n_pairs, t, w2), F32)
    mat_spec = pl.BlockSpec((None, cpb, w2, w2), lambda bi, p, ti: (bi * n_pairs + p, ti, 0, 0))
    slab_spec = pl.BlockSpec((None, tc, w2), lambda bi, p, ti: (p, bi * n_tb + ti, 0))
    m_mat, g_mat, rp, y0, bonus, gate = pl.pallas_call(
        functools.partial(_rwkv_chunk_kernel, tc=tc, n_head=n_head),
        out_shape=[mat_shape, mat_shape, slab_shape, slab_shape, slab_shape, slab_shape],
        grid=(b, n_pairs, n_tb),
        in_specs=in_specs,
        out_specs=[mat_spec, mat_spec, slab_spec, slab_spec, slab_spec, slab_spec],
        scratch_shapes=[pltpu.VMEM((tc, w2), F32)] * 3 + [pltpu.VMEM((tc, lw), F32)],
        compiler_params=_params(("parallel", "parallel", "arbitrary")),
        name="rwkv_chunk",
    )(z, z, z, z, z, z, z, z, mu2, mu2, mu2, mu2, row(w0), row(a0), row(k_k), row(k_a), row(r_k),
      w_up, a_up, g_up)

    n_pb = n_pairs // pairs_blk
    wblk = pairs_blk * w2
    mat_in = pl.BlockSpec((pairs_blk, 1, w2, w2), lambda bi, pb, ci: (bi * n_pb + pb, ci, 0, 0))
    slab_in = pl.BlockSpec((pairs_blk, ch, w2), lambda bi, pb, ci: (pb, bi * n_chunks + ci, 0))
    par_in = pl.BlockSpec((pairs_blk, 1, w2), lambda bi, pb, ci: (pb, 0, 0))
    by_pair = lambda v_: v_.reshape(n_pairs, 1, w2)
    return pl.pallas_call(
        functools.partial(_rwkv_state_kernel, n_pairs_blk=pairs_blk, n_head=n_head),
        out_shape=jax.ShapeDtypeStruct((t, c), BF16),
        grid=(b, n_pb, n_chunks),
        in_specs=[mat_in, mat_in, slab_in, slab_in, slab_in, slab_in, par_in, par_in],
        out_specs=pl.BlockSpec((ch, wblk), lambda bi, pb, ci: (bi * n_chunks + ci, pb)),
        scratch_shapes=[pltpu.VMEM((pairs_blk, w2, w2), F32)],
        compiler_params=_params(("parallel", "parallel", "arbitrary")),
        name="rwkv_state",
    )(m_mat, g_mat, rp, y0, bonus, gate, by_pair(ln_w), by_pair(ln_b))


def _router_kernel(x_ref, g_ref, w_ref, b_ref, h_ref, logit_ref, *, eps):
    x = x_ref[...]
    ms = jnp.mean(x * x, axis=-1, keepdims=True)
    h = x * lax.rsqrt(ms + eps) * g_ref[...]
    h_ref[...] = h
    logit_ref[...] = _dot_x3(h, w_ref[...]) + b_ref[...]


def _router(x, g, w_router, b_router, tm=256):
    t, d = x.shape
    n = w_router.shape[1]
    return pl.pallas_call(
        functools.partial(_router_kernel, eps=RMS_EPS),
        out_shape=[jax.ShapeDtypeStruct((t, d), F32), jax.ShapeDtypeStruct((t, n), F32)],
        grid=(t // tm,),
        in_specs=[pl.BlockSpec((tm, d), lambda i: (i, 0)), pl.BlockSpec((1, d), lambda i: (0, 0)),
                  pl.BlockSpec((d, n), lambda i: (0, 0)), pl.BlockSpec((1, n), lambda i: (0, 0))],
        out_specs=[pl.BlockSpec((tm, d), lambda i: (i, 0)), pl.BlockSpec((tm, n), lambda i: (i, 0))],
        compiler_params=_params(("parallel",)),
        name="moe_router",
    )(x, g.reshape(1, d), w_router, b_router)


def _gather_rows(idx_ref, base, src_hbm, dst_vmem, sem, n_rows):
    def issue(r, _):
        pltpu.make_async_copy(src_hbm.at[pl.ds(idx_ref[base + r], 1), :], dst_vmem.at[pl.ds(r, 1), :], sem).start()
        return 0
    lax.fori_loop(0, n_rows, issue, 0, unroll=DMA_ISSUE_UNROLL)


def _wait_rows(src_hbm, dst_vmem, sem, n_rows):
    pltpu.make_async_copy(src_hbm.at[pl.ds(0, n_rows), :], dst_vmem, sem).wait()


def _dispatch_kernel(tok_ref, n_used_ref, h_hbm, o_ref, buf, sem, *, blk):
    i = pl.program_id(0)
    n_used = n_used_ref[0]

    def issue(step):
        slot = step & 1
        _gather_rows(tok_ref, step * blk, h_hbm, buf.at[slot], sem.at[slot], blk)

    @pl.when((i == 0) & (n_used > 0))
    def _():
        issue(i)

    @pl.when(i + 1 < n_used)
    def _():
        issue(i + 1)

    @pl.when(i < n_used)
    def _():
        slot = i & 1
        _wait_rows(h_hbm, buf.at[slot], sem.at[slot], blk)
        o_ref[...] = buf[slot].astype(o_ref.dtype)

    @pl.when(i >= n_used)
    def _():
        o_ref[...] = jnp.zeros_like(o_ref)


def _dispatch(slot_tok, n_used, h, blk):
    n_slots = slot_tok.shape[0]
    d = h.shape[1]
    return pl.pallas_call(
        functools.partial(_dispatch_kernel, blk=blk),
        out_shape=jax.ShapeDtypeStruct((n_slots, d), BF16),
        grid_spec=pltpu.PrefetchScalarGridSpec(
            num_scalar_prefetch=2,
            grid=(n_slots // blk,),
            in_specs=[pl.BlockSpec(memory_space=pl.ANY)],
            out_specs=pl.BlockSpec((blk, d), lambda i, tok, nu: (i, 0)),
            scratch_shapes=[pltpu.VMEM((2, blk, d), F32), pltpu.SemaphoreType.DMA((2,))],
        ),
        compiler_params=_params(("arbitrary",)),
        name="moe_dispatch",
    )(slot_tok, n_used, h)


def _expert_up_kernel(blk_e_ref, n_used_ref, x_ref, wg_ref, wu_ref, o_ref):
    used = pl.program_id(1) < n_used_ref[0]

    @pl.when(used)
    def _():
        x = x_ref[...]
        gate = _dot(x, wg_ref[...].astype(BF16))
        up = _dot(x, wu_ref[...].astype(BF16))
        o_ref[...] = (gate * _sigmoid(gate) * up).astype(o_ref.dtype)

    @pl.when(jnp.logical_not(used))
    def _():
        o_ref[...] = jnp.zeros_like(o_ref)


def _expert_down_kernel(blk_e_ref, n_used_ref, h_ref, wd_ref, o_ref):
    used = pl.program_id(0) < n_used_ref[0]

    @pl.when(used)
    def _():
        o_ref[...] = _dot(h_ref[...], wd_ref[...].astype(BF16))

    @pl.when(jnp.logical_not(used))
    def _():
        o_ref[...] = jnp.zeros_like(o_ref)


def _experts(blk_e, n_used, xs, wg, wu, wd, layer, blk, tn=256):
    n_slots, d = xs.shape
    de = wg.shape[-1]
    n_blk = n_slots // blk
    tn = min(tn, de)
    used = lambda i, nu: jnp.minimum(i, nu[0] - 1)
    w_up_spec = pl.BlockSpec((None, None, d, tn), lambda j, i, be, nu: (layer, be[used(i, nu)], 0, j))
    hid = pl.pallas_call(
        _expert_up_kernel,
        out_shape=jax.ShapeDtypeStruct((n_slots, de), BF16),
        grid_spec=pltpu.PrefetchScalarGridSpec(
            num_scalar_prefetch=2,
            grid=(de // tn, n_blk),
            in_specs=[pl.BlockSpec((blk, d), lambda j, i, be, nu: (used(i, nu), 0)), w_up_spec, w_up_spec],
            out_specs=pl.BlockSpec((blk, tn), lambda j, i, be, nu: (i, j)),
        ),
        compiler_params=_params(("parallel", "arbitrary")),
        name="moe_expert_up",
    )(blk_e, n_used, xs, wg, wu)
    return pl.pallas_call(
        _expert_down_kernel,
        out_shape=jax.ShapeDtypeStruct((n_slots, d), F32),
        grid_spec=pltpu.PrefetchScalarGridSpec(
            num_scalar_prefetch=2,
            grid=(n_blk,),
            in_specs=[pl.BlockSpec((blk, de), lambda i, be, nu: (used(i, nu), 0)),
                      pl.BlockSpec((None, None, de, d), lambda i, be, nu: (layer, be[used(i, nu)], 0, 0))],
            out_specs=pl.BlockSpec((blk, d), lambda i, be, nu: (i, 0)),
        ),
        compiler_params=_params(("arbitrary",)),
        name="moe_expert_down",
    )(blk_e, n_used, hid, wd)


def _combine_kernel(pos0_ref, pos1_ref, x_ref, w_ref, g_ref, y_hbm, *refs, tb, n_steps, emit_x, eps):
    outs, (buf0, buf1, sem0, sem1) = refs[:-4], refs[-4:]
    i = pl.program_id(0)

    def issue(step):
        slot = step & 1
        _gather_rows(pos0_ref, step * tb, y_hbm, buf0.at[slot], sem0.at[slot], tb)
        _gather_rows(pos1_ref, step * tb, y_hbm, buf1.at[slot], sem1.at[slot], tb)

    @pl.when(i == 0)
    def _():
        issue(i)

    @pl.when(i + 1 < n_steps)
    def _():
        issue(i + 1)

    slot = i & 1
    w = w_ref[...]
    _wait_rows(y_hbm, buf0.at[slot], sem0.at[slot], tb)
    _wait_rows(y_hbm, buf1.at[slot], sem1.at[slot], tb)
    x = x_ref[...] + (buf0[slot] * w[:, 0:1] + buf1[slot] * w[:, 1:2])
    if emit_x:
        outs[0][...] = x
    ms = jnp.mean(x * x, axis=-1, keepdims=True)
    outs[-1][...] = (x * lax.rsqrt(ms + eps) * g_ref[...]).astype(outs[-1].dtype)


def _combine(pos0, pos1, x, wts, y, gain, norm_dtype, emit_x, tb=256):
    t, d = x.shape
    row_spec = pl.BlockSpec((tb, d), lambda i, p0, p1: (i, 0))
    out_shape = [jax.ShapeDtypeStruct((t, d), norm_dtype)]
    if emit_x:
        out_shape.insert(0, jax.ShapeDtypeStruct((t, d), F32))
    return pl.pallas_call(
        functools.partial(_combine_kernel, tb=tb, n_steps=t // tb, emit_x=emit_x, eps=RMS_EPS),
        out_shape=out_shape,
        grid_spec=pltpu.PrefetchScalarGridSpec(
            num_scalar_prefetch=2,
            grid=(t // tb,),
            in_specs=[row_spec,
                      pl.BlockSpec((tb, TOP_K), lambda i, p0, p1: (i, 0)),
                      pl.BlockSpec((1, d), lambda i, p0, p1: (0, 0)),
                      pl.BlockSpec(memory_space=pl.ANY)],
            out_specs=[row_spec] * len(out_shape),
            scratch_shapes=[pltpu.VMEM((2, tb, d), F32), pltpu.VMEM((2, tb, d), F32),
                            pltpu.SemaphoreType.DMA((2,)), pltpu.SemaphoreType.DMA((2,))],
        ),
        compiler_params=_params(("arbitrary",)),
        name="moe_combine",
    )(pos0, pos1, x, wts, gain.reshape(1, d), y)


def _route(logits, n_groups, n_experts, blk):
    t = logits.shape[0]
    epg = n_experts // n_groups
    g_logits = logits[:, :n_groups]
    p_group = jax.nn.softmax(g_logits, axis=-1)
    g_idx = jnp.argmax(g_logits, axis=-1)
    p_sel = jnp.take_along_axis(p_group, g_idx[:, None], axis=1)
    e_logits = logits[:, n_groups:n_groups + n_experts].reshape(t, n_groups, epg)
    e_sel = jnp.take_along_axis(e_logits, g_idx[:, None, None], axis=1)[:, 0]
    top_v, top_i = lax.top_k(e_sel, TOP_K)
    wts = jax.nn.softmax(top_v, axis=-1) * p_sel
    eid = (g_idx[:, None] * epg + top_i).astype(jnp.int32)
    e_flat = eid.reshape(-1)
    n_assign = e_flat.shape[0]
    onehot = (e_flat[:, None] == jnp.arange(n_experts, dtype=jnp.int32)[None, :]).astype(jnp.int32)
    csum = jnp.cumsum(onehot, axis=0)
    counts = csum[-1]
    rank = jnp.take_along_axis(csum, e_flat[:, None], axis=1)[:, 0] - 1
    padded = (counts + blk - 1) // blk * blk
    pends = jnp.cumsum(padded)
    pstarts = pends - padded
    pos = (pstarts[e_flat] + rank).astype(jnp.int32)
    n_blk = (n_assign + blk - 1) // blk + n_experts
    tok_flat = jnp.repeat(jnp.arange(t, dtype=jnp.int32), TOP_K)
    slot_tok = jnp.zeros((n_blk * blk,), jnp.int32).at[pos].set(tok_flat)
    blk_start = jnp.arange(n_blk, dtype=jnp.int32) * blk
    blk_e = jnp.minimum(jnp.sum(blk_start[:, None] >= pends[None, :], axis=1), n_experts - 1).astype(jnp.int32)
    pos2 = pos.reshape(t, TOP_K)
    n_used = (pends[-1:] // blk).astype(jnp.int32)
    return slot_tok, blk_e, n_used, pos2[:, 0], pos2[:, 1], wts.astype(F32)


def _hier_moe(x, ffn_norm, w_rg, b_rg, w_re, b_re, wg, wu, wd, layer, next_gain, next_dtype, emit_x):
    n_groups, n_experts = w_rg.shape[1], w_re.shape[1]
    n_r = n_groups + n_experts
    n_pad = -n_r % LANES
    w_router = jnp.pad(jnp.concatenate([w_rg, w_re], axis=1), ((0, 0), (0, n_pad)))
    b_router = jnp.pad(jnp.concatenate([b_rg, b_re], axis=0), (0, n_pad)).reshape(1, -1)
    h, logits = _router(x, ffn_norm, w_router, b_router)
    slot_tok, blk_e, n_used, pos0, pos1, wts = _route(logits, n_groups, n_experts, MOE_BLK)
    xs = _dispatch(slot_tok, n_used, h, MOE_BLK)
    y = _experts(blk_e, n_used, xs, wg, wu, wd, layer, MOE_BLK)
    return _combine(pos0, pos1, x, wts, y, next_gain, next_dtype, emit_x)


def kernel(x, attn_norm, w_in, diff_lambda, diff_head_norm, rwkv_mu, rwkv_w0, rwkv_w_up, rwkv_a0, rwkv_a_up, rwkv_g_up, rwkv_k_k, rwkv_k_a, rwkv_r_k, rwkv_ln_w, rwkv_ln_b, w_out, ffn_norm, router_group, router_group_bias, router_expert, router_expert_bias, expert_w_gate, expert_w_up, expert_w_down, final_norm):
    b, s, d = x.shape
    depth = w_in.shape[0]
    t = b * s
    dv = diff_head_norm.shape[-1]
    rw = rwkv_w0.shape[-1]
    dw = d - rw
    n_heads = dw // dv
    attn_cols = 3 * dw
    rwkv_cols = w_in.shape[-1] - attn_cols
    slopes = jnp.asarray([LOG2E * 2.0 ** (-8.0 * (i + 1) / n_heads) for i in range(n_heads)], F32)
    tm = min(1024, t)

    xt = x.reshape(t, d)
    h = _rmsnorm(xt, attn_norm[0], BF16)
    for l in range(depth):
        lam_init = 0.8 - 0.6 * math.exp(-0.3 * l)
        last = l == depth - 1
        proj_a = _matmul([h], w_in, l, 0, attn_cols, BF16, tm, _tile(attn_cols, (512, 256, 128)), name="in_proj_attn")
        proj_r = _matmul([h], w_in, l, attn_cols, rwkv_cols, F32, tm, 256, name="in_proj_rwkv")
        y_diff = _diff_attention(proj_a, diff_lambda[l], diff_head_norm[l], slopes, b, s, n_heads, lam_init)
        y_rwkv = _rwkv7(proj_r, rwkv_mu[l], rwkv_w0[l], rwkv_w_up[l], rwkv_a0[l], rwkv_a_up[l], rwkv_g_up[l],
                        rwkv_k_k[l], rwkv_k_a[l], rwkv_r_k[l], rwkv_ln_w[l], rwkv_ln_b[l], b, s)
        xt = _matmul([y_diff, y_rwkv], w_out, l, 0, d, F32, tm, _tile(d, (512, 256, 128)), res=xt, name="out_proj")
        outs = _hier_moe(xt, ffn_norm[l], router_group[l], router_group_bias[l], router_expert[l],
                         router_expert_bias[l], expert_w_gate, expert_w_up, expert_w_down, l,
                         final_norm if last else attn_norm[l + 1], F32 if last else BF16, not last)
        if not last:
            xt, h = outs
    return outs[0].reshape(b, s, d)
```

```python
import functools
import math

import jax
import jax.numpy as jnp
from jax import lax
from jax.experimental import pallas as pl
from jax.experimental.pallas import tpu as pltpu

F32 = jnp.float32
BF16 = jnp.bfloat16

LANES = 128
VMEM_BUDGET_BYTES = 56 << 20

MASK_CHUNK = 64
RMS_EPS = 1e-6
HEAD_NORM_EPS = 1e-5
GN_EPS = 64e-5
DECAY_SCALE = math.exp(-0.5)
TOP_K = 2
LORA_W = 64
LORA_A = 64
LORA_G = 128

ATTN_BLOCK = 512
LOG2E = math.log2(math.e)
RWKV_CHUNK = 64
RWKV_GROUP = 8
RWKV_BLOCK = 2048
MOE_BLK = 256
DMA_ISSUE_UNROLL = 8


def _tile(n, candidates):
    return next(c for c in candidates if n % c == 0)


def _params(semantics):
    return pltpu.CompilerParams(dimension_semantics=semantics, vmem_limit_bytes=VMEM_BUDGET_BYTES)


def _dot(a, b):
    return jnp.dot(a, b, preferred_element_type=F32)


def _dot_nt(a, b):
    return lax.dot_general(a, b, (((1,), (1,)), ((), ())), preferred_element_type=F32)


def _dot_tn(a, b):
    return lax.dot_general(a, b, (((0,), (0,)), ((), ())), preferred_element_type=F32)


def _split2(x):
    hi = x.astype(BF16)
    lo = (x - hi.astype(F32)).astype(BF16)
    return hi, lo


def _split3(x):
    hi = x.astype(BF16)
    r1 = x - hi.astype(F32)
    mid = r1.astype(BF16)
    lo = (r1 - mid.astype(F32)).astype(BF16)
    return hi, mid, lo


def _dot_x3(a, b):
    ah, al = _split2(a)
    bh, bl = _split2(b)
    return _dot(ah, bh) + _dot(ah, bl) + _dot(al, bh)


def _iota(shape, dim):
    return lax.broadcasted_iota(jnp.int32, shape, dim)


def _sigmoid(x):
    return 1.0 / (1.0 + jnp.exp(-x))


def _rmsnorm_kernel(x_ref, g_ref, o_ref, *, eps):
    x = x_ref[...]
    ms = jnp.mean(x * x, axis=-1, keepdims=True)
    o_ref[...] = (x * lax.rsqrt(ms + eps) * g_ref[...]).astype(o_ref.dtype)


def _rmsnorm(x, g, out_dtype, tm=256):
    t, d = x.shape
    return pl.pallas_call(
        functools.partial(_rmsnorm_kernel, eps=RMS_EPS),
        out_shape=jax.ShapeDtypeStruct((t, d), out_dtype),
        grid=(t // tm,),
        in_specs=[pl.BlockSpec((tm, d), lambda i: (i, 0)), pl.BlockSpec((1, d), lambda i: (0, 0))],
        out_specs=pl.BlockSpec((tm, d), lambda i: (i, 0)),
        compiler_params=_params(("parallel",)),
        name="rmsnorm",
    )(x, g.reshape(1, d))


def _mm_kernel(*refs, n_a, has_res):
    a_refs, w_refs = refs[:n_a], refs[n_a:2 * n_a]
    o_ref = refs[-1]
    acc = None
    for a_ref, w_ref in zip(a_refs, w_refs):
        d = _dot(a_ref[...], w_ref[...].astype(BF16))
        acc = d if acc is None else acc + d
    if has_res:
        acc = acc + refs[2 * n_a][...]
    o_ref[...] = acc.astype(o_ref.dtype)


def _matmul(a_list, w, layer, col_off, n_out, out_dtype, tm, tn, res=None, name="matmul"):
    m, ka = a_list[0].shape
    n_a = len(a_list)
    assert w.shape[1] == ka * n_a and col_off % tn == 0 and n_out % tn == 0 and m % tm == 0
    in_specs = [pl.BlockSpec((tm, ka), lambda i, j: (i, 0)) for _ in a_list]
    in_specs += [pl.BlockSpec((None, ka, tn),
                              functools.partial(lambda i, j, r, c: (layer, r, j + c), r=r, c=col_off // tn))
                 for r in range(n_a)]
    args = list(a_list) + [w] * n_a
    if res is not None:
        in_specs.append(pl.BlockSpec((tm, tn), lambda i, j: (i, j)))
        args.append(res)
    return pl.pallas_call(
        functools.partial(_mm_kernel, n_a=n_a, has_res=res is not None),
        out_shape=jax.ShapeDtypeStruct((m, n_out), out_dtype),
        grid=(m // tm, n_out // tn),
        in_specs=in_specs,
        out_specs=pl.BlockSpec((tm, tn), lambda i, j: (i, j)),
        compiler_params=_params(("parallel", "arbitrary")),
        name=name,
    )(*args)


def _attn_kernel(slopes_ref, lam_ref, diag_ref, q_ref, k_ref, v_ref, hn_ref, o_ref, *, tq, tk, dk, lam_init):
    h = pl.program_id(1)
    qi = pl.program_id(2)
    slope = slopes_ref[h]
    q0 = qi * tq
    dv = q_ref.shape[1]

    lam_p = lam_ref[...]
    lam = (jnp.exp(jnp.sum(lam_p[0:1] * lam_p[1:2], keepdims=True))
           - jnp.exp(jnp.sum(lam_p[2:3] * lam_p[3:4], keepdims=True)) + lam_init)

    q = q_ref[...] * (dk ** -0.5 * LOG2E)
    lane = _iota((tq, dv), 1)
    zero = jnp.zeros_like(q)
    q2 = jnp.concatenate([jnp.where(lane < dk, q, zero), jnp.where(lane >= dk, q, zero)], axis=0)

    def blk(i):
        return pl.multiple_of(i * tk, tk)

    def scores(k0):
        return _dot_nt(q2, k_ref[pl.ds(k0, tk), :])

    def update(s, k0, masked, carry):
        v = v_ref[pl.ds(k0, tk), :]
        m, l, acc = carry
        if masked:
            bias = slope * diag_ref[...]
            s = s + jnp.concatenate([bias, bias], axis=0)
        else:
            s = s + slope * (k0 - q0 + _iota((1, tk), 1)).astype(F32)
        m_new = jnp.maximum(m, jnp.max(s, axis=-1, keepdims=True))
        alpha = jnp.exp2(m - m_new)
        p = jnp.exp2(s - m_new)
        l = alpha * l + jnp.sum(p, axis=-1, keepdims=True)
        acc = alpha * acc + _dot(p.astype(BF16), v)
        return m_new, l, acc

    carry = (jnp.full((2 * tq, 1), -jnp.inf, F32), jnp.zeros((2 * tq, 1), F32), jnp.zeros((2 * tq, dv), F32))
    def pair(j, c):
        ka, kb = blk(2 * j), blk(2 * j + 1)
        sa, sb = scores(ka), scores(kb)
        return update(sb, kb, False, update(sa, ka, False, c))

    def tail_odd(c):
        ka, kb = blk(qi - 1), blk(qi)
        sa, sb = scores(ka), scores(kb)
        return update(sb, kb, True, update(sa, ka, False, c))

    def tail_even(c):
        return update(scores(blk(qi)), blk(qi), True, c)

    carry = lax.fori_loop(0, jnp.right_shift(qi, 1), pair, carry)
    carry = lax.cond((qi & 1) == 1, tail_odd, tail_even, carry)
    _, l, acc = carry
    o = acc / l
    o = o[:tq] - lam * o[tq:]
    ms = jnp.mean(o * o, axis=-1, keepdims=True)
    o = o * lax.rsqrt(ms + HEAD_NORM_EPS) * hn_ref[...]
    o_ref[...] = (o * (1.0 - lam_init)).astype(o_ref.dtype)


def _diff_attention(proj, lam_p, head_norm, slopes, b, s, n_heads, lam_init):
    t = proj.shape[0]
    dv = head_norm.shape[-1]
    dk = lam_p.shape[-1]
    tq = tk = min(ATTN_BLOCK, s)
    assert s % tq == 0 and tq % MASK_CHUNK == 0
    nq = s // tq
    r, c = jnp.arange(tq)[:, None], jnp.arange(tk)[None, :]
    diag = jnp.where(c // MASK_CHUNK <= r // MASK_CHUNK, (r - jnp.abs(r - c)).astype(F32), -jnp.inf)
    return pl.pallas_call(
        functools.partial(_attn_kernel, tq=tq, tk=tk, dk=dk, lam_init=lam_init),
        out_shape=jax.ShapeDtypeStruct((t, n_heads * dv), BF16),
        grid=(b, n_heads, nq),
        in_specs=[
            pl.BlockSpec(memory_space=pltpu.SMEM),
            pl.BlockSpec(lam_p.shape, lambda bi, h, qi: (0, 0)),
            pl.BlockSpec((tq, tk), lambda bi, h, qi: (0, 0)),
            pl.BlockSpec((tq, dv), lambda bi, h, qi: (bi * nq + qi, h)),
            pl.BlockSpec((s, dv), lambda bi, h, qi: (bi, n_heads + h)),
            pl.BlockSpec((s, dv), lambda bi, h, qi: (bi, 2 * n_heads + h)),
            pl.BlockSpec((1, dv), lambda bi, h, qi: (0, 0)),
        ],
        out_specs=pl.BlockSpec((tq, dv), lambda bi, h, qi: (bi * nq + qi, h)),
        compiler_params=_params(("parallel", "parallel", "arbitrary")),
        name="diff_attention",
    )(slopes, lam_p, diag, proj, proj, proj, head_norm.reshape(1, dv))


def _rwkv_chunk_kernel(zr_ref, zk_ref, zv_ref, zl_ref, hr_ref, hk_ref, hv_ref, hl_ref,
                       mur_ref, muk_ref, muv_ref, mul_ref, w0_ref, a0_ref, kk_ref, ka_ref, rk_ref,
                       wup_ref, aup_ref, gup_ref,
                       m_ref, g_ref, rp_ref, y0_ref, bonus_ref, gate_ref,
                       sr, sk, sv, sl, *, tc, n_head):
    c = RWKV_CHUNK
    n_chunks = tc // c
    first = pl.program_id(2) == 0
    w2 = 2 * n_head
    pt = 2 * c

    def shifted(z_ref, h_ref, mu_ref, out_ref):
        z = z_ref[...]
        prev_row = jnp.where(first, 0.0, h_ref[7:8, :])
        zp = pltpu.roll(z, 1, axis=0)
        zp = jnp.where(_iota(z.shape, 0) == 0, prev_row, zp)
        out_ref[...] = z + mu_ref[...] * (zp - z)

    shifted(zr_ref, hr_ref, mur_ref, sr)
    shifted(zk_ref, hk_ref, muk_ref, sk)
    shifted(zv_ref, hv_ref, muv_ref, sv)
    shifted(zl_ref, hl_ref, mul_ref, sl)

    grp = min(RWKV_GROUP, n_chunks)
    n = grp * c
    m0 = _iota((c, w2), 1) < n_head
    m0n = _iota((n, w2), 1) < n_head
    row_p, col_p = _iota((pt, pt), 0), _iota((pt, pt), 1)
    same_head = (row_p // c) == (col_p // c)
    strict = same_head & ((row_p % c) > (col_p % c))
    incl = same_head & ((row_p % c) >= (col_p % c))
    eye_p = (row_p == col_p).astype(F32)
    row_l, col_l = _iota((w2, w2), 0), _iota((w2, w2), 1)
    head_blk = (row_l // n_head) == (col_l // n_head)
    seg_ones = head_blk.astype(BF16)
    row_n, col_n = _iota((n, n), 0), _iota((n, n), 1)
    tri = (((row_n // c) == (col_n // c)) & (row_n >= col_n)).astype(BF16)
    w_cat = jnp.concatenate([wup_ref[...], aup_ref[...]], axis=0)
    g_up = gup_ref[...].astype(BF16)

    def seg_sum(x):
        hi, lo = _split2(x)
        return _dot(hi, seg_ones) + _dot(lo, seg_ones)

    def bmm(a, b):
        return lax.dot_general(a, b, (((2,), (1,)), ((0,), (0,))), preferred_element_type=F32)

    def bmm_nt(a, b):
        return lax.dot_general(a, b, (((2,), (2,)), ((0,), (0,))), preferred_element_type=F32)

    def bmm_tn(a, b):
        return lax.dot_general(a, b, (((1,), (1,)), ((0,), (0,))), preferred_element_type=F32)

    def by_chunk(x):
        return x.reshape(grp, c, x.shape[-1])

    def stack(x):
        z = jnp.zeros_like(x)
        return jnp.concatenate([jnp.where(m0, x, z), jnp.where(m0, z, x)], axis=1)

    def unstack(x):
        return x[:, :c] + x[:, c:]

    PREP, SOLVE = 0, 1

    def chunk_group(gi):
        rows = slice(gi * n, (gi + 1) * n)
        r, k, v, lora = sr[rows, :], sk[rows, :], sv[rows, :], sl[rows, :]
        d_wa = lora[:, :LORA_W + LORA_A]
        zero = jnp.zeros_like(d_wa)
        logit_w = w0_ref[...] + _dot_x3(jnp.where(m0n, jnp.tanh(d_wa), zero), w_cat)
        logit_a = a0_ref[...] + _dot_x3(jnp.where(m0n, zero, d_wa), w_cat)
        yield PREP
        logw = -DECAY_SCALE * _sigmoid(logit_w)
        a = _sigmoid(logit_a)
        kk = k * kk_ref[...]
        kap = kk / jnp.maximum(jnp.sqrt(seg_sum(kk * kk)), 1e-12)
        yield PREP
        kh = k * (1.0 + (a - 1.0) * ka_ref[...])
        bvec = kap * a
        lw_h, lw_m, lw_l = _split3(logw)
        cum = _dot(tri, lw_h) + _dot(tri, lw_m) + _dot(tri, lw_l)
        yield PREP
        cum3 = by_chunk(cum)
        cum_end = cum3[:, c - 1:c, :]
        inv_dec = jnp.exp(-cum)
        kap_t = stack(by_chunk(kap * jnp.exp(cum - logw)))
        yield PREP
        r_t = stack(by_chunk(r * jnp.exp(cum)))
        lhs = jnp.concatenate([kap_t, r_t], axis=1).astype(BF16)
        yield PREP
        rhs = jnp.concatenate([stack(by_chunk(bvec * inv_dec)), stack(by_chunk(kh * inv_dec))],
                              axis=1).astype(BF16)
        yield PREP
        v3 = by_chunk(v)
        v_s = stack(v3).astype(BF16)
        dec_out = jnp.exp(cum_end - cum3)
        b_out = (by_chunk(bvec) * dec_out).astype(BF16)
        k_out = (by_chunk(kh) * dec_out).astype(BF16)
        yield PREP
        gate_ref[rows, :] = _dot(_sigmoid(lora[:, LORA_W + LORA_A:]).astype(BF16), g_up)
        bonus_ref[rows, :] = seg_sum(r * kh * rk_ref[...]) * v
        yield PREP

        aa = bmm_nt(lhs, rhs)
        kv = bmm_tn(k_out, v3.astype(BF16))
        yield SOLVE
        a_kb = jnp.where(strict, aa[:, :pt, :pt], 0.0)
        a_kk = jnp.where(strict, aa[:, :pt, pt:], 0.0)
        a_rb = jnp.where(incl, aa[:, pt:, :pt], 0.0).astype(BF16)
        a_rk = jnp.where(incl, aa[:, pt:, pt:], 0.0).astype(BF16)

        x = a_kb
        tinv = eye_p - x
        for _ in range(int(math.log2(c)) - 1):
            xb = x.astype(BF16)
            x = bmm(xb, xb)
            yield SOLVE
            tinv = bmm(tinv.astype(BF16), (eye_p + x).astype(BF16))
            yield SOLVE

        akv = bmm(a_kk.astype(BF16), v_s)
        yield SOLVE
        pq = bmm(tinv.astype(BF16), jnp.concatenate([kap_t, akv], axis=2).astype(BF16))
        yield SOLVE
        arb_pq = bmm(a_rb, pq.astype(BF16))
        rp = r_t - arb_pq[:, :, :w2]
        yield SOLVE
        y0 = bmm(a_rk, v_s) - arb_pq[:, :, w2:]
        pq_s = unstack(pq).astype(BF16)
        yield SOLVE
        bt_pq = bmm_tn(b_out, pq_s)
        m_mat = eye_p[:w2, :w2] * jnp.exp(cum_end) - jnp.where(head_blk, bt_pq[:, :, :w2], 0.0)
        g_mat = jnp.where(head_blk, kv - bt_pq[:, :, w2:], 0.0)

        chunks = slice(gi * grp, (gi + 1) * grp)
        m_ref[chunks] = m_mat
        g_ref[chunks] = g_mat
        rp_ref[rows, :] = unstack(rp).reshape(n, w2)
        y0_ref[rows, :] = unstack(y0).reshape(n, w2)
        yield SOLVE

    def run_prep(gen, max_stages):
        for _ in range(max_stages):
            if next(gen) == SOLVE:
                return False
        return True

    groups = [chunk_group(gi) for gi in range(n_chunks // grp)]
    run_prep(groups[0], 1 << 30)
    for gi, gen in enumerate(groups):
        nxt = groups[gi + 1] if gi + 1 < len(groups) else None
        nxt_in_prep = nxt is not None
        for _ in gen:
            if nxt_in_prep:
                nxt_in_prep = run_prep(nxt, 1)
        if nxt_in_prep:
            run_prep(nxt, 1 << 30)


def _rwkv_state_kernel(m_ref, g_ref, rp_ref, y0_ref, bonus_ref, gate_ref, lnw_ref, lnb_ref, o_ref, st_ref,
                       *, n_pairs_blk, n_head):
    @pl.when(pl.program_id(2) == 0)
    def _():
        st_ref[...] = jnp.zeros_like(st_ref)

    w2 = 2 * n_head
    head_blk = ((_iota((w2, w2), 0) // n_head) == (_iota((w2, w2), 1) // n_head)).astype(BF16)

    def seg_mean(x):
        hi, lo = _split2(x)
        return (_dot(hi, head_blk) + _dot(lo, head_blk)) * (1.0 / n_head)

    def bmm(a, b):
        return lax.dot_general(a, b, (((2,), (1,)), ((0,), (0,))), preferred_element_type=F32)

    ch = rp_ref.shape[1]
    st_hi, st_lo = _split2(st_ref[...])
    rp = rp_ref[...].astype(BF16)
    y = bmm(rp, st_hi) + bmm(rp, st_lo) + y0_ref[...]
    m = m_ref[:, 0].astype(BF16)
    st_ref[...] = bmm(m, st_hi) + bmm(m, st_lo) + g_ref[:, 0]
    y = y.reshape(n_pairs_blk * ch, w2)
    mean = seg_mean(y)
    yc = y - mean
    var = seg_mean(yc * yc)
    yn = (yc * lax.rsqrt(var + GN_EPS)).reshape(n_pairs_blk, ch, w2)
    out = (yn * lnw_ref[...] + lnb_ref[...] + bonus_ref[...]) * gate_ref[...]
    for p in range(n_pairs_blk):
        o_ref[:, p * w2:(p + 1) * w2] = out[p].astype(o_ref.dtype)


def _rwkv7(z, mu, w0, w_up, a0, a_up, g_up, k_k, k_a, r_k, ln_w, ln_b, b, s, tc=RWKV_BLOCK, pairs_blk=16):
    t = z.shape[0]
    c = w0.shape[-1]
    n_head = r_k.shape[-1]
    w2 = 2 * n_head
    assert w2 == LANES and (3 * c) % (2 * LANES) == 0 and LORA_W + LORA_A == LANES
    n_pairs = c // w2
    tc = min(tc, s)
    pairs_blk = min(pairs_blk, n_pairs)
    n_tb = s // tc
    ch = RWKV_CHUNK
    n_chunks = s // ch
    cpb = tc // ch
    lw = LORA_W + LORA_A + LORA_G
    cb = c // w2
    lb = 3 * c // lw
    mu2 = mu.reshape(1, -1)
    row = lambda v_: v_.reshape(1, c)

    def zspec(col_blk_fn, width):
        return pl.BlockSpec((tc, width), lambda bi, p, ti: (bi * n_tb + ti, col_blk_fn(p)))

    def hspec(col_blk_fn, width):
        return pl.BlockSpec((8, width),
                            lambda bi, p, ti: (jnp.maximum((bi * n_tb + ti) * (tc // 8) - 1, 0), col_blk_fn(p)))

    def pspec(col_blk_fn, width, rows=1):
        return pl.BlockSpec((rows, width), lambda bi, p, ti: (0, col_blk_fn(p)))

    sec = [lambda p: p, lambda p: cb + p, lambda p: 2 * cb + p]
    in_specs = ([zspec(f, w2) for f in sec] + [zspec(lambda p: lb, lw)]
                + [hspec(f, w2) for f in sec] + [hspec(lambda p: lb, lw)]
                + [pspec(f, w2) for f in sec] + [pspec(lambda p: lb, lw)]
                + [pspec(sec[0], w2) for _ in range(5)]
                + [pspec(sec[0], w2, LORA_W), pspec(sec[0], w2, LORA_A), pspec(sec[0], w2, LORA_G)])
    mat_shape = jax.ShapeDtypeStruct((b * n_pairs, n_chunks, w2, w2), F32)
    slab_shape = jax.ShapeDtypeStruct((n_pairs, t, w2), F32)
    mat_spec = pl.BlockSpec((None, cpb, w2, w2), lambda bi, p, ti: (bi * n_pairs + p, ti, 0, 0))
    slab_spec = pl.BlockSpec((None, tc, w2), lambda bi, p, ti: (p, bi * n_tb + ti, 0))
    m_mat, g_mat, rp, y0, bonus, gate = pl.pallas_call(
        functools.partial(_rwkv_chunk_kernel, tc=tc, n_head=n_head),
        out_shape=[mat_shape, mat_shape, slab_shape, slab_shape, slab_shape, slab_shape],
        grid=(b, n_pairs, n_tb),
        in_specs=in_specs,
        out_specs=[mat_spec, mat_spec, slab_spec, slab_spec, slab_spec, slab_spec],
        scratch_shapes=[pltpu.VMEM((tc, w2), F32)] * 3 + [pltpu.VMEM((tc, lw), F32)],
        compiler_params=_params(("parallel", "parallel", "arbitrary")),
        name="rwkv_chunk",
    )(z, z, z, z, z, z, z, z, mu2, mu2, mu2, mu2, row(w0), row(a0), row(k_k), row(k_a), row(r_k),
      w_up, a_up, g_up)

    n_pb = n_pairs // pairs_blk
    wblk = pairs_blk * w2
    mat_in = pl.BlockSpec((pairs_blk, 1, w2, w2), lambda bi, pb, ci: (bi * n_pb + pb, ci, 0, 0))
    slab_in = pl.BlockSpec((pairs_blk, ch, w2), lambda bi, pb, ci: (pb, bi * n_chunks + ci, 0))
    par_in = pl.BlockSpec((pairs_blk, 1, w2), lambda bi, pb, ci: (pb, 0, 0))
    by_pair = lambda v_: v_.reshape(n_pairs, 1, w2)
    return pl.pallas_call(
        functools.partial(_rwkv_state_kernel, n_pairs_blk=pairs_blk, n_head=n_head),
        out_shape=jax.ShapeDtypeStruct((t, c), BF16),
        grid=(b, n_pb, n_chunks),
        in_specs=[mat_in, mat_in, slab_in, slab_in, slab_in, slab_in, par_in, par_in],
        out_specs=pl.BlockSpec((ch, wblk), lambda bi, pb, ci: (bi * n_chunks + ci, pb)),
        scratch_shapes=[pltpu.VMEM((pairs_blk, w2, w2), F32)],
        compiler_params=_params(("parallel", "parallel", "arbitrary")),
        name="rwkv_state",
    )(m_mat, g_mat, rp, y0, bonus, gate, by_pair(ln_w), by_pair(ln_b))


def _pitch(d):
    return d // LANES + 1


def _store_vectors(ref, x):
    n, d = x.shape
    pitch = _pitch(d)
    for c in range(d // LANES):
        ref[pl.ds(c, n, stride=pitch), :] = x[:, c * LANES:(c + 1) * LANES]
    ref[pl.ds(pitch - 1, n, stride=pitch), :] = jnp.zeros((n, LANES), x.dtype)


def _load_vectors(ref, n, d):
    pitch = _pitch(d)
    return jnp.concatenate([ref[pl.ds(c, n, stride=pitch), :] for c in range(d // LANES)], axis=1)


def _router_kernel(x_ref, g_ref, w_ref, b_ref, h_ref, logit_ref, *, eps):
    x = x_ref[...]
    ms = jnp.mean(x * x, axis=-1, keepdims=True)
    h = x * lax.rsqrt(ms + eps) * g_ref[...]
    _store_vectors(h_ref, h)
    logit_ref[...] = _dot_x3(h, w_ref[...]) + b_ref[...]


def _router(x, g, w_router, b_router, tm=256):
    t, d = x.shape
    n = w_router.shape[1]
    pitch = _pitch(d)
    return pl.pallas_call(
        functools.partial(_router_kernel, eps=RMS_EPS),
        out_shape=[jax.ShapeDtypeStruct((t * pitch, LANES), F32), jax.ShapeDtypeStruct((t, n), F32)],
        grid=(t // tm,),
        in_specs=[pl.BlockSpec((tm, d), lambda i: (i, 0)), pl.BlockSpec((1, d), lambda i: (0, 0)),
                  pl.BlockSpec((d, n), lambda i: (0, 0)), pl.BlockSpec((1, n), lambda i: (0, 0))],
        out_specs=[pl.BlockSpec((tm * pitch, LANES), lambda i: (i, 0)), pl.BlockSpec((tm, n), lambda i: (i, 0))],
        compiler_params=_params(("parallel",)),
        name="moe_router",
    )(x, g.reshape(1, d), w_router, b_router)


def _gather_rows(idx_ref, base, src_hbm, dst_vmem, sem, n_rows, d):
    pitch, rows = _pitch(d), d // LANES

    def issue(r, _):
        pltpu.make_async_copy(src_hbm.at[pl.ds(idx_ref[base + r] * pitch, rows), :],
                              dst_vmem.at[pl.ds(r * pitch, rows), :], sem).start()
        return 0
    lax.fori_loop(0, n_rows, issue, 0, unroll=DMA_ISSUE_UNROLL)


def _wait_rows(src_hbm, dst_vmem, sem, n_rows, d):
    total = n_rows * (d // LANES)
    pltpu.make_async_copy(src_hbm.at[pl.ds(0, total), :], dst_vmem.at[pl.ds(0, total), :], sem).wait()


def _dispatch_kernel(tok_ref, n_used_ref, h_hbm, o_ref, buf, sem, *, blk):
    i = pl.program_id(0)
    n_used = n_used_ref[0]
    d = o_ref.shape[1]

    def issue(step):
        slot = step & 1
        _gather_rows(tok_ref, step * blk, h_hbm, buf.at[slot], sem.at[slot], blk, d)

    @pl.when((i == 0) & (n_used > 0))
    def _():
        issue(i)

    @pl.when(i + 1 < n_used)
    def _():
        issue(i + 1)

    @pl.when(i < n_used)
    def _():
        slot = i & 1
        _wait_rows(h_hbm, buf.at[slot], sem.at[slot], blk, d)
        o_ref[...] = _load_vectors(buf.at[slot], blk, d).astype(o_ref.dtype)

    @pl.when(i >= n_used)
    def _():
        o_ref[...] = jnp.zeros_like(o_ref)


def _dispatch(slot_tok, n_used, h, d, blk):
    n_slots = slot_tok.shape[0]
    return pl.pallas_call(
        functools.partial(_dispatch_kernel, blk=blk),
        out_shape=jax.ShapeDtypeStruct((n_slots, d), BF16),
        grid_spec=pltpu.PrefetchScalarGridSpec(
            num_scalar_prefetch=2,
            grid=(n_slots // blk,),
            in_specs=[pl.BlockSpec(memory_space=pl.ANY)],
            out_specs=pl.BlockSpec((blk, d), lambda i, tok, nu: (i, 0)),
            scratch_shapes=[pltpu.VMEM((2, blk * _pitch(d), LANES), F32), pltpu.SemaphoreType.DMA((2,))],
        ),
        compiler_params=_params(("arbitrary",)),
        name="moe_dispatch",
    )(slot_tok, n_used, h)


def _expert_up_kernel(blk_e_ref, n_used_ref, x_ref, wg_ref, wu_ref, o_ref):
    used = pl.program_id(1) < n_used_ref[0]

    @pl.when(used)
    def _():
        x = x_ref[...]
        gate = _dot(x, wg_ref[...].astype(BF16))
        up = _dot(x, wu_ref[...].astype(BF16))
        o_ref[...] = (gate * _sigmoid(gate) * up).astype(o_ref.dtype)

    @pl.when(jnp.logical_not(used))
    def _():
        o_ref[...] = jnp.zeros_like(o_ref)


def _expert_down_kernel(blk_e_ref, n_used_ref, h_ref, wd_ref, o_ref):
    used = pl.program_id(0) < n_used_ref[0]

    @pl.when(used)
    def _():
        _store_vectors(o_ref, _dot(h_ref[...], wd_ref[...].astype(BF16)))

    @pl.when(jnp.logical_not(used))
    def _():
        o_ref[...] = jnp.zeros_like(o_ref)


def _experts(blk_e, n_used, xs, wg, wu, wd, layer, blk, tn=256):
    n_slots, d = xs.shape
    de = wg.shape[-1]
    n_blk = n_slots // blk
    tn = min(tn, de)
    used = lambda i, nu: jnp.minimum(i, nu[0] - 1)
    w_up_spec = pl.BlockSpec((None, None, d, tn), lambda j, i, be, nu: (layer, be[used(i, nu)], 0, j))
    hid = pl.pallas_call(
        _expert_up_kernel,
        out_shape=jax.ShapeDtypeStruct((n_slots, de), BF16),
        grid_spec=pltpu.PrefetchScalarGridSpec(
            num_scalar_prefetch=2,
            grid=(de // tn, n_blk),
            in_specs=[pl.BlockSpec((blk, d), lambda j, i, be, nu: (used(i, nu), 0)), w_up_spec, w_up_spec],
            out_specs=pl.BlockSpec((blk, tn), lambda j, i, be, nu: (i, j)),
        ),
        compiler_params=_params(("parallel", "arbitrary")),
        name="moe_expert_up",
    )(blk_e, n_used, xs, wg, wu)
    pitch = _pitch(d)
    return pl.pallas_call(
        _expert_down_kernel,
        out_shape=jax.ShapeDtypeStruct((n_slots * pitch, LANES), F32),
        grid_spec=pltpu.PrefetchScalarGridSpec(
            num_scalar_prefetch=2,
            grid=(n_blk,),
            in_specs=[pl.BlockSpec((blk, de), lambda i, be, nu: (used(i, nu), 0)),
                      pl.BlockSpec((None, None, de, d), lambda i, be, nu: (layer, be[used(i, nu)], 0, 0))],
            out_specs=pl.BlockSpec((blk * pitch, LANES), lambda i, be, nu: (i, 0)),
        ),
        compiler_params=_params(("arbitrary",)),
        name="moe_expert_down",
    )(blk_e, n_used, hid, wd)


def _combine_kernel(pos0_ref, pos1_ref, x_ref, w_ref, g_ref, y_hbm, *refs, tb, n_steps, emit_x, eps):
    outs, (buf0, buf1, sem0, sem1) = refs[:-4], refs[-4:]
    i = pl.program_id(0)
    d = x_ref.shape[1]

    def issue(step):
        slot = step & 1
        _gather_rows(pos0_ref, step * tb, y_hbm, buf0.at[slot], sem0.at[slot], tb, d)
        _gather_rows(pos1_ref, step * tb, y_hbm, buf1.at[slot], sem1.at[slot], tb, d)

    @pl.when(i == 0)
    def _():
        issue(i)

    @pl.when(i + 1 < n_steps)
    def _():
        issue(i + 1)

    slot = i & 1
    w = w_ref[...]
    _wait_rows(y_hbm, buf0.at[slot], sem0.at[slot], tb, d)
    _wait_rows(y_hbm, buf1.at[slot], sem1.at[slot], tb, d)
    y0 = _load_vectors(buf0.at[slot], tb, d)
    y1 = _load_vectors(buf1.at[slot], tb, d)
    x = x_ref[...] + (y0 * w[:, 0:1] + y1 * w[:, 1:2])
    if emit_x:
        outs[0][...] = x
    ms = jnp.mean(x * x, axis=-1, keepdims=True)
    outs[-1][...] = (x * lax.rsqrt(ms + eps) * g_ref[...]).astype(outs[-1].dtype)


def _combine(pos0, pos1, x, wts, y, gain, norm_dtype, emit_x, tb=256):
    t, d = x.shape
    row_spec = pl.BlockSpec((tb, d), lambda i, p0, p1: (i, 0))
    out_shape = [jax.ShapeDtypeStruct((t, d), norm_dtype)]
    if emit_x:
        out_shape.insert(0, jax.ShapeDtypeStruct((t, d), F32))
    return pl.pallas_call(
        functools.partial(_combine_kernel, tb=tb, n_steps=t // tb, emit_x=emit_x, eps=RMS_EPS),
        out_shape=out_shape,
        grid_spec=pltpu.PrefetchScalarGridSpec(
            num_scalar_prefetch=2,
            grid=(t // tb,),
            in_specs=[row_spec,
                      pl.BlockSpec((tb, TOP_K), lambda i, p0, p1: (i, 0)),
                      pl.BlockSpec((1, d), lambda i, p0, p1: (0, 0)),
                      pl.BlockSpec(memory_space=pl.ANY)],
            out_specs=[row_spec] * len(out_shape),
            scratch_shapes=[pltpu.VMEM((2, tb * _pitch(d), LANES), F32), pltpu.VMEM((2, tb * _pitch(d), LANES), F32),
                            pltpu.SemaphoreType.DMA((2,)), pltpu.SemaphoreType.DMA((2,))],
        ),
        compiler_params=_params(("arbitrary",)),
        name="moe_combine",
    )(pos0, pos1, x, wts, gain.reshape(1, d), y)


def _route(logits, n_groups, n_experts, blk):
    t = logits.shape[0]
    epg = n_experts // n_groups
    g_logits = logits[:, :n_groups]
    p_group = jax.nn.softmax(g_logits, axis=-1)
    g_idx = jnp.argmax(g_logits, axis=-1)
    p_sel = jnp.take_along_axis(p_group, g_idx[:, None], axis=1)
    e_logits = logits[:, n_groups:n_groups + n_experts].reshape(t, n_groups, epg)
    e_sel = jnp.take_along_axis(e_logits, g_idx[:, None, None], axis=1)[:, 0]
    top_v, top_i = lax.top_k(e_sel, TOP_K)
    wts = jax.nn.softmax(top_v, axis=-1) * p_sel
    eid = (g_idx[:, None] * epg + top_i).astype(jnp.int32)
    e_flat = eid.reshape(-1)
    n_assign = e_flat.shape[0]
    onehot = (e_flat[:, None] == jnp.arange(n_experts, dtype=jnp.int32)[None, :]).astype(jnp.int32)
    csum = jnp.cumsum(onehot, axis=0)
    counts = csum[-1]
    rank = jnp.take_along_axis(csum, e_flat[:, None], axis=1)[:, 0] - 1
    padded = (counts + blk - 1) // blk * blk
    pends = jnp.cumsum(padded)
    pstarts = pends - padded
    pos = (pstarts[e_flat] + rank).astype(jnp.int32)
    n_blk = (n_assign + blk - 1) // blk + n_experts
    tok_flat = jnp.repeat(jnp.arange(t, dtype=jnp.int32), TOP_K)
    slot_tok = jnp.zeros((n_blk * blk,), jnp.int32).at[pos].set(tok_flat)
    blk_start = jnp.arange(n_blk, dtype=jnp.int32) * blk
    blk_e = jnp.minimum(jnp.sum(blk_start[:, None] >= pends[None, :], axis=1), n_experts - 1).astype(jnp.int32)
    pos2 = pos.reshape(t, TOP_K)
    n_used = (pends[-1:] // blk).astype(jnp.int32)
    return slot_tok, blk_e, n_used, pos2[:, 0], pos2[:, 1], wts.astype(F32)


def _hier_moe(x, ffn_norm, w_rg, b_rg, w_re, b_re, wg, wu, wd, layer, next_gain, next_dtype, emit_x):
    n_groups, n_experts = w_rg.shape[1], w_re.shape[1]
    n_r = n_groups + n_experts
    n_pad = -n_r % LANES
    w_router = jnp.pad(jnp.concatenate([w_rg, w_re], axis=1), ((0, 0), (0, n_pad)))
    b_router = jnp.pad(jnp.concatenate([b_rg, b_re], axis=0), (0, n_pad)).reshape(1, -1)
    h, logits = _router(x, ffn_norm, w_router, b_router)
    slot_tok, blk_e, n_used, pos0, pos1, wts = _route(logits, n_groups, n_experts, MOE_BLK)
    xs = _dispatch(slot_tok, n_used, h, x.shape[1], MOE_BLK)
    y = _experts(blk_e, n_used, xs, wg, wu, wd, layer, MOE_BLK)
    return _combine(pos0, pos1, x, wts, y, next_gain, next_dtype, emit_x)


def kernel(x, attn_norm, w_in, diff_lambda, diff_head_norm, rwkv_mu, rwkv_w0, rwkv_w_up, rwkv_a0, rwkv_a_up, rwkv_g_up, rwkv_k_k, rwkv_k_a, rwkv_r_k, rwkv_ln_w, rwkv_ln_b, w_out, ffn_norm, router_group, router_group_bias, router_expert, router_expert_bias, expert_w_gate, expert_w_up, expert_w_down, final_norm):
    b, s, d = x.shape
    depth = w_in.shape[0]
    t = b * s
    dv = diff_head_norm.shape[-1]
    rw = rwkv_w0.shape[-1]
    dw = d - rw
    n_heads = dw // dv
    attn_cols = 3 * dw
    rwkv_cols = w_in.shape[-1] - attn_cols
    slopes = jnp.asarray([LOG2E * 2.0 ** (-8.0 * (i + 1) / n_heads) for i in range(n_heads)], F32)
    tm = min(1024, t)

    xt = x.reshape(t, d)
    h = _rmsnorm(xt, attn_norm[0], BF16)
    for l in range(depth):
        lam_init = 0.8 - 0.6 * math.exp(-0.3 * l)
        last = l == depth - 1
        proj_a = _matmul([h], w_in, l, 0, attn_cols, BF16, tm, _tile(attn_cols, (512, 256, 128)), name="in_proj_attn")
        proj_r = _matmul([h], w_in, l, attn_cols, rwkv_cols, F32, tm, 256, name="in_proj_rwkv")
        y_diff = _diff_attention(proj_a, diff_lambda[l], diff_head_norm[l], slopes, b, s, n_heads, lam_init)
        y_rwkv = _rwkv7(proj_r, rwkv_mu[l], rwkv_w0[l], rwkv_w_up[l], rwkv_a0[l], rwkv_a_up[l], rwkv_g_up[l],
                        rwkv_k_k[l], rwkv_k_a[l], rwkv_r_k[l], rwkv_ln_w[l], rwkv_ln_b[l], b, s)
        xt = _matmul([y_diff, y_rwkv], w_out, l, 0, d, F32, tm, _tile(d, (512, 256, 128)), res=xt, name="out_proj")
        outs = _hier_moe(xt, ffn_norm[l], router_group[l], router_group_bias[l], router_expert[l],
                         router_expert_bias[l], expert_w_gate, expert_w_up, expert_w_down, l,
                         final_norm if last else attn_norm[l + 1], F32 if last else BF16, not last)
        if not last:
            xt, h = outs
    return outs[0].reshape(b, s, d)
```

```python
import functools
import math

import jax
import jax.numpy as jnp
from jax import lax
from jax.experimental import pallas as pl
from jax.experimental.pallas import tpu as pltpu

F32 = jnp.float32
BF16 = jnp.bfloat16

LANES = 128
VMEM_BUDGET_BYTES = 56 << 20

MASK_CHUNK = 64
RMS_EPS = 1e-6
HEAD_NORM_EPS = 1e-5
GN_EPS = 64e-5
DECAY_SCALE = math.exp(-0.5)
TOP_K = 2
LORA_W = 64
LORA_A = 64
LORA_G = 128

ATTN_BLOCK = 512
LOG2E = math.log2(math.e)
RWKV_CHUNK = 64
RWKV_GROUP = 8
RWKV_BLOCK = 2048
MOE_BLK = 256
DMA_ISSUE_UNROLL = 8


def _tile(n, candidates):
    return next(c for c in candidates if n % c == 0)


def _params(semantics):
    return pltpu.CompilerParams(dimension_semantics=semantics, vmem_limit_bytes=VMEM_BUDGET_BYTES)


def _dot(a, b):
    return jnp.dot(a, b, preferred_element_type=F32)


def _dot_nt(a, b):
    return lax.dot_general(a, b, (((1,), (1,)), ((), ())), preferred_element_type=F32)


def _dot_tn(a, b):
    return lax.dot_general(a, b, (((0,), (0,)), ((), ())), preferred_element_type=F32)


def _split2(x):
    hi = x.astype(BF16)
    lo = (x - hi.astype(F32)).astype(BF16)
    return hi, lo


def _split3(x):
    hi = x.astype(BF16)
    r1 = x - hi.astype(F32)
    mid = r1.astype(BF16)
    lo = (r1 - mid.astype(F32)).astype(BF16)
    return hi, mid, lo


def _dot_x3(a, b):
    ah, al = _split2(a)
    bh, bl = _split2(b)
    return _dot(ah, bh) + _dot(ah, bl) + _dot(al, bh)


def _iota(shape, dim):
    return lax.broadcasted_iota(jnp.int32, shape, dim)


def _sigmoid(x):
    return 1.0 / (1.0 + jnp.exp(-x))


def _rmsnorm_kernel(x_ref, g_ref, o_ref, *, eps):
    x = x_ref[...]
    ms = jnp.mean(x * x, axis=-1, keepdims=True)
    o_ref[...] = (x * lax.rsqrt(ms + eps) * g_ref[...]).astype(o_ref.dtype)


def _rmsnorm(x, g, out_dtype, tm=256):
    t, d = x.shape
    return pl.pallas_call(
        functools.partial(_rmsnorm_kernel, eps=RMS_EPS),
        out_shape=jax.ShapeDtypeStruct((t, d), out_dtype),
        grid=(t // tm,),
        in_specs=[pl.BlockSpec((tm, d), lambda i: (i, 0)), pl.BlockSpec((1, d), lambda i: (0, 0))],
        out_specs=pl.BlockSpec((tm, d), lambda i: (i, 0)),
        compiler_params=_params(("parallel",)),
        name="rmsnorm",
    )(x, g.reshape(1, d))


def _mm_kernel(*refs, n_a, has_res):
    a_refs, w_refs = refs[:n_a], refs[n_a:2 * n_a]
    o_ref = refs[-1]
    acc = None
    for a_ref, w_ref in zip(a_refs, w_refs):
        d = _dot(a_ref[...], w_ref[...].astype(BF16))
        acc = d if acc is None else acc + d
    if has_res:
        acc = acc + refs[2 * n_a][...]
    o_ref[...] = acc.astype(o_ref.dtype)


def _matmul(a_list, w, layer, col_off, n_out, out_dtype, tm, tn, res=None, name="matmul"):
    m, ka = a_list[0].shape
    n_a = len(a_list)
    assert w.shape[1] == ka * n_a and col_off % tn == 0 and n_out % tn == 0 and m % tm == 0
    in_specs = [pl.BlockSpec((tm, ka), lambda i, j: (i, 0)) for _ in a_list]
    in_specs += [pl.BlockSpec((None, ka, tn),
                              functools.partial(lambda i, j, r, c: (layer, r, j + c), r=r, c=col_off // tn))
                 for r in range(n_a)]
    args = list(a_list) + [w] * n_a
    if res is not None:
        in_specs.append(pl.BlockSpec((tm, tn), lambda i, j: (i, j)))
        args.append(res)
    return pl.pallas_call(
        functools.partial(_mm_kernel, n_a=n_a, has_res=res is not None),
        out_shape=jax.ShapeDtypeStruct((m, n_out), out_dtype),
        grid=(m // tm, n_out // tn),
        in_specs=in_specs,
        out_specs=pl.BlockSpec((tm, tn), lambda i, j: (i, j)),
        compiler_params=_params(("parallel", "arbitrary")),
        name=name,
    )(*args)


def _attn_kernel(slopes_ref, lam_ref, diag_ref, q_ref, k_ref, v_ref, hn_ref, o_ref, *, tq, tk, dk, lam_init):
    h = pl.program_id(1)
    qi = pl.program_id(2)
    slope = slopes_ref[h]
    q0 = qi * tq
    dv = q_ref.shape[1]

    lam_p = lam_ref[...]
    lam = (jnp.exp(jnp.sum(lam_p[0:1] * lam_p[1:2], keepdims=True))
           - jnp.exp(jnp.sum(lam_p[2:3] * lam_p[3:4], keepdims=True)) + lam_init)

    q = q_ref[...] * (dk ** -0.5 * LOG2E)
    lane = _iota((tq, dv), 1)
    zero = jnp.zeros_like(q)
    q2 = jnp.concatenate([jnp.where(lane < dk, q, zero), jnp.where(lane >= dk, q, zero)], axis=0)

    def blk(i):
        return pl.multiple_of(i * tk, tk)

    def scores(k0):
        return _dot_nt(q2, k_ref[pl.ds(k0, tk), :])

    def update(s, k0, masked, carry):
        v = v_ref[pl.ds(k0, tk), :]
        m, l, acc = carry
        if masked:
            bias = slope * diag_ref[...]
            s = s + jnp.concatenate([bias, bias], axis=0)
        else:
            s = s + slope * (k0 - q0 + _iota((1, tk), 1)).astype(F32)
        m_new = jnp.maximum(m, jnp.max(s, axis=-1, keepdims=True))
        alpha = jnp.exp2(m - m_new)
        p = jnp.exp2(s - m_new)
        l = alpha * l + jnp.sum(p, axis=-1, keepdims=True)
        acc = alpha * acc + _dot(p.astype(BF16), v)
        return m_new, l, acc

    carry = (jnp.full((2 * tq, 1), -jnp.inf, F32), jnp.zeros((2 * tq, 1), F32), jnp.zeros((2 * tq, dv), F32))
    def pair(j, c):
        ka, kb = blk(2 * j), blk(2 * j + 1)
        sa, sb = scores(ka), scores(kb)
        return update(sb, kb, False, update(sa, ka, False, c))

    def tail_odd(c):
        ka, kb = blk(qi - 1), blk(qi)
        sa, sb = scores(ka), scores(kb)
        return update(sb, kb, True, update(sa, ka, False, c))

    def tail_even(c):
        return update(scores(blk(qi)), blk(qi), True, c)

    carry = lax.fori_loop(0, jnp.right_shift(qi, 1), pair, carry)
    carry = lax.cond((qi & 1) == 1, tail_odd, tail_even, carry)
    _, l, acc = carry
    o = acc / l
    o = o[:tq] - lam * o[tq:]
    ms = jnp.mean(o * o, axis=-1, keepdims=True)
    o = o * lax.rsqrt(ms + HEAD_NORM_EPS) * hn_ref[...]
    o_ref[...] = (o * (1.0 - lam_init)).astype(o_ref.dtype)


def _diff_attention(proj, lam_p, head_norm, slopes, b, s, n_heads, lam_init):
    t = proj.shape[0]
    dv = head_norm.shape[-1]
    dk = lam_p.shape[-1]
    tq = tk = min(ATTN_BLOCK, s)
    assert s % tq == 0 and tq % MASK_CHUNK == 0
    nq = s // tq
    r, c = jnp.arange(tq)[:, None], jnp.arange(tk)[None, :]
    diag = jnp.where(c // MASK_CHUNK <= r // MASK_CHUNK, (r - jnp.abs(r - c)).astype(F32), -jnp.inf)
    return pl.pallas_call(
        functools.partial(_attn_kernel, tq=tq, tk=tk, dk=dk, lam_init=lam_init),
        out_shape=jax.ShapeDtypeStruct((t, n_heads * dv), BF16),
        grid=(b, n_heads, nq),
        in_specs=[
            pl.BlockSpec(memory_space=pltpu.SMEM),
            pl.BlockSpec(lam_p.shape, lambda bi, h, qi: (0, 0)),
            pl.BlockSpec((tq, tk), lambda bi, h, qi: (0, 0)),
            pl.BlockSpec((tq, dv), lambda bi, h, qi: (bi * nq + qi, h)),
            pl.BlockSpec((s, dv), lambda bi, h, qi: (bi, n_heads + h)),
            pl.BlockSpec((s, dv), lambda bi, h, qi: (bi, 2 * n_heads + h)),
            pl.BlockSpec((1, dv), lambda bi, h, qi: (0, 0)),
        ],
        out_specs=pl.BlockSpec((tq, dv), lambda bi, h, qi: (bi * nq + qi, h)),
        compiler_params=_params(("parallel", "parallel", "arbitrary")),
        name="diff_attention",
    )(slopes, lam_p, diag, proj, proj, proj, head_norm.reshape(1, dv))


def _rwkv_chunk_kernel(zr_ref, zk_ref, zv_ref, zl_ref, hr_ref, hk_ref, hv_ref, hl_ref,
                       mur_ref, muk_ref, muv_ref, mul_ref, w0_ref, a0_ref, kk_ref, ka_ref, rk_ref,
                       wup_ref, aup_ref, gup_ref,
                       m_ref, g_ref, rp_ref, y0_ref, bonus_ref, gate_ref,
                       sr, sk, sv, sl, *, tc, n_head):
    c = RWKV_CHUNK
    n_chunks = tc // c
    first = pl.program_id(2) == 0
    w2 = 2 * n_head
    pt = 2 * c

    def shifted(z_ref, h_ref, mu_ref, out_ref):
        z = z_ref[...]
        prev_row = jnp.where(first, 0.0, h_ref[7:8, :])
        zp = pltpu.roll(z, 1, axis=0)
        zp = jnp.where(_iota(z.shape, 0) == 0, prev_row, zp)
        out_ref[...] = z + mu_ref[...] * (zp - z)

    shifted(zr_ref, hr_ref, mur_ref, sr)
    shifted(zk_ref, hk_ref, muk_ref, sk)
    shifted(zv_ref, hv_ref, muv_ref, sv)
    shifted(zl_ref, hl_ref, mul_ref, sl)

    grp = min(RWKV_GROUP, n_chunks)
    n = grp * c
    m0 = _iota((c, w2), 1) < n_head
    m0n = _iota((n, w2), 1) < n_head
    row_p, col_p = _iota((pt, pt), 0), _iota((pt, pt), 1)
    same_head = (row_p // c) == (col_p // c)
    strict = same_head & ((row_p % c) > (col_p % c))
    incl = same_head & ((row_p % c) >= (col_p % c))
    eye_p = (row_p == col_p).astype(F32)
    row_l, col_l = _iota((w2, w2), 0), _iota((w2, w2), 1)
    head_blk = (row_l // n_head) == (col_l // n_head)
    seg_ones = head_blk.astype(BF16)
    tri = (_iota((grp, c, c), 1) >= _iota((grp, c, c), 2)).astype(BF16)
    w_cat = jnp.concatenate([wup_ref[...], aup_ref[...]], axis=0)
    g_up = gup_ref[...].astype(BF16)

    def seg_sum(x):
        hi, lo = _split2(x)
        return _dot(hi, seg_ones) + _dot(lo, seg_ones)

    def bmm(a, b):
        return lax.dot_general(a, b, (((2,), (1,)), ((0,), (0,))), preferred_element_type=F32)

    def bmm_nt(a, b):
        return lax.dot_general(a, b, (((2,), (2,)), ((0,), (0,))), preferred_element_type=F32)

    def bmm_tn(a, b):
        return lax.dot_general(a, b, (((1,), (1,)), ((0,), (0,))), preferred_element_type=F32)

    def by_chunk(x):
        return x.reshape(grp, c, x.shape[-1])

    def stack(x):
        z = jnp.zeros_like(x)
        return jnp.concatenate([jnp.where(m0, x, z), jnp.where(m0, z, x)], axis=1)

    def unstack(x):
        return x[:, :c] + x[:, c:]

    PREP, SOLVE = 0, 1

    def chunk_group(gi):
        rows = slice(gi * n, (gi + 1) * n)
        r, k, v, lora = sr[rows, :], sk[rows, :], sv[rows, :], sl[rows, :]
        d_wa = lora[:, :LORA_W + LORA_A]
        zero = jnp.zeros_like(d_wa)
        logit_w = w0_ref[...] + _dot_x3(jnp.where(m0n, jnp.tanh(d_wa), zero), w_cat)
        logit_a = a0_ref[...] + _dot_x3(jnp.where(m0n, zero, d_wa), w_cat)
        yield PREP
        logw = -DECAY_SCALE * _sigmoid(logit_w)
        a = _sigmoid(logit_a)
        kk = k * kk_ref[...]
        kap = kk / jnp.maximum(jnp.sqrt(seg_sum(kk * kk)), 1e-12)
        yield PREP
        kh = k * (1.0 + (a - 1.0) * ka_ref[...])
        bvec = kap * a
        lw_h, lw_m, lw_l = _split3(logw)
        cum3 = bmm(tri, by_chunk(lw_h)) + bmm(tri, by_chunk(lw_m)) + bmm(tri, by_chunk(lw_l))
        cum = cum3.reshape(n, w2)
        yield PREP
        cum_end = cum3[:, c - 1:c, :]
        inv_dec = jnp.exp(-cum)
        kap_t = stack(by_chunk(kap * jnp.exp(cum - logw)))
        yield PREP
        r_t = stack(by_chunk(r * jnp.exp(cum)))
        lhs = jnp.concatenate([kap_t, r_t], axis=1).astype(BF16)
        yield PREP
        rhs = jnp.concatenate([stack(by_chunk(bvec * inv_dec)), stack(by_chunk(kh * inv_dec))],
                              axis=1).astype(BF16)
        yield PREP
        v3 = by_chunk(v)
        v_s = stack(v3).astype(BF16)
        dec_out = jnp.exp(cum_end - cum3)
        b_out = (by_chunk(bvec) * dec_out).astype(BF16)
        k_out = (by_chunk(kh) * dec_out).astype(BF16)
        yield PREP
        gate_ref[rows, :] = _dot(_sigmoid(lora[:, LORA_W + LORA_A:]).astype(BF16), g_up)
        bonus_ref[rows, :] = seg_sum(r * kh * rk_ref[...]) * v
        yield PREP

        aa = bmm_nt(lhs, rhs)
        kv = bmm_tn(k_out, v3.astype(BF16))
        yield SOLVE
        a_kb = jnp.where(strict, aa[:, :pt, :pt], 0.0)
        a_kk = jnp.where(strict, aa[:, :pt, pt:], 0.0)
        a_rb = jnp.where(incl, aa[:, pt:, :pt], 0.0).astype(BF16)
        a_rk = jnp.where(incl, aa[:, pt:, pt:], 0.0).astype(BF16)

        x = a_kb
        tinv = eye_p - x
        for _ in range(int(math.log2(c)) - 1):
            xb = x.astype(BF16)
            x = bmm(xb, xb)
            yield SOLVE
            tinv = bmm(tinv.astype(BF16), (eye_p + x).astype(BF16))
            yield SOLVE

        akv = bmm(a_kk.astype(BF16), v_s)
        yield SOLVE
        pq = bmm(tinv.astype(BF16), jnp.concatenate([kap_t, akv], axis=2).astype(BF16))
        yield SOLVE
        arb_pq = bmm(a_rb, pq.astype(BF16))
        rp = r_t - arb_pq[:, :, :w2]
        yield SOLVE
        y0 = bmm(a_rk, v_s) - arb_pq[:, :, w2:]
        pq_s = unstack(pq).astype(BF16)
        yield SOLVE
        bt_pq = bmm_tn(b_out, pq_s)
        m_mat = eye_p[:w2, :w2] * jnp.exp(cum_end) - jnp.where(head_blk, bt_pq[:, :, :w2], 0.0)
        g_mat = jnp.where(head_blk, kv - bt_pq[:, :, w2:], 0.0)

        chunks = slice(gi * grp, (gi + 1) * grp)
        m_ref[chunks] = m_mat
        g_ref[chunks] = g_mat
        rp_ref[rows, :] = unstack(rp).reshape(n, w2)
        y0_ref[rows, :] = unstack(y0).reshape(n, w2)
        yield SOLVE

    def run_prep(gen, max_stages):
        for _ in range(max_stages):
            if next(gen) == SOLVE:
                return False
        return True

    groups = [chunk_group(gi) for gi in range(n_chunks // grp)]
    run_prep(groups[0], 1 << 30)
    for gi, gen in enumerate(groups):
        nxt = groups[gi + 1] if gi + 1 < len(groups) else None
        nxt_in_prep = nxt is not None
        for _ in gen:
            if nxt_in_prep:
                nxt_in_prep = run_prep(nxt, 1)
        if nxt_in_prep:
            run_prep(nxt, 1 << 30)


def _rwkv_state_kernel(m_ref, g_ref, rp_ref, y0_ref, bonus_ref, gate_ref, lnw_ref, lnb_ref, o_ref, st_ref,
                       *, n_pairs_blk, n_head):
    @pl.when(pl.program_id(2) == 0)
    def _():
        st_ref[...] = jnp.zeros_like(st_ref)

    w2 = 2 * n_head
    head_blk = ((_iota((w2, w2), 0) // n_head) == (_iota((w2, w2), 1) // n_head)).astype(BF16)

    def seg_mean(x):
        hi, lo = _split2(x)
        return (_dot(hi, head_blk) + _dot(lo, head_blk)) * (1.0 / n_head)

    def bmm(a, b):
        return lax.dot_general(a, b, (((2,), (1,)), ((0,), (0,))), preferred_element_type=F32)

    ch = rp_ref.shape[1]
    st_hi, st_lo = _split2(st_ref[...])
    rp = rp_ref[...].astype(BF16)
    y = bmm(rp, st_hi) + bmm(rp, st_lo) + y0_ref[...]
    m = m_ref[:, 0].astype(BF16)
    st_ref[...] = bmm(m, st_hi) + bmm(m, st_lo) + g_ref[:, 0]
    y = y.reshape(n_pairs_blk * ch, w2)
    mean = seg_mean(y)
    yc = y - mean
    var = seg_mean(yc * yc)
    yn = (yc * lax.rsqrt(var + GN_EPS)).reshape(n_pairs_blk, ch, w2)
    out = (yn * lnw_ref[...] + lnb_ref[...] + bonus_ref[...]) * gate_ref[...]
    for p in range(n_pairs_blk):
        o_ref[:, p * w2:(p + 1) * w2] = out[p].astype(o_ref.dtype)


def _rwkv7(z, mu, w0, w_up, a0, a_up, g_up, k_k, k_a, r_k, ln_w, ln_b, b, s, tc=RWKV_BLOCK, pairs_blk=16):
    t = z.shape[0]
    c = w0.shape[-1]
    n_head = r_k.shape[-1]
    w2 = 2 * n_head
    assert w2 == LANES and (3 * c) % (2 * LANES) == 0 and LORA_W + LORA_A == LANES
    n_pairs = c // w2
    tc = min(tc, s)
    pairs_blk = min(pairs_blk, n_pairs)
    n_tb = s // tc
    ch = RWKV_CHUNK
    n_chunks = s // ch
    cpb = tc // ch
    lw = LORA_W + LORA_A + LORA_G
    cb = c // w2
    lb = 3 * c // lw
    mu2 = mu.reshape(1, -1)
    row = lambda v_: v_.reshape(1, c)

    def zspec(col_blk_fn, width):
        return pl.BlockSpec((tc, width), lambda bi, p, ti: (bi * n_tb + ti, col_blk_fn(p)))

    def hspec(col_blk_fn, width):
        return pl.BlockSpec((8, width),
                            lambda bi, p, ti: (jnp.maximum((bi * n_tb + ti) * (tc // 8) - 1, 0), col_blk_fn(p)))

    def pspec(col_blk_fn, width, rows=1):
        return pl.BlockSpec((rows, width), lambda bi, p, ti: (0, col_blk_fn(p)))

    sec = [lambda p: p, lambda p: cb + p, lambda p: 2 * cb + p]
    in_specs = ([zspec(f, w2) for f in sec] + [zspec(lambda p: lb, lw)]
                + [hspec(f, w2) for f in sec] + [hspec(lambda p: lb, lw)]
                + [pspec(f, w2) for f in sec] + [pspec(lambda p: lb, lw)]
                + [pspec(sec[0], w2) for _ in range(5)]
                + [pspec(sec[0], w2, LORA_W), pspec(sec[0], w2, LORA_A), pspec(sec[0], w2, LORA_G)])
    mat_shape = jax.ShapeDtypeStruct((b * n_pairs, n_chunks, w2, w2), F32)
    slab_shape = jax.ShapeDtypeStruct((n_pairs, t, w2), F32)
    mat_spec = pl.BlockSpec((None, cpb, w2, w2), lambda bi, p, ti: (bi * n_pairs + p, ti, 0, 0))
    slab_spec = pl.BlockSpec((None, tc, w2), lambda bi, p, ti: (p, bi * n_tb + ti, 0))
    m_mat, g_mat, rp, y0, bonus, gate = pl.pallas_call(
        functools.partial(_rwkv_chunk_kernel, tc=tc, n_head=n_head),
        out_shape=[mat_shape, mat_shape, slab_shape, slab_shape, slab_shape, slab_shape],
        grid=(b, n_pairs, n_tb),
        in_specs=in_specs,
        out_specs=[mat_spec, mat_spec, slab_spec, slab_spec, slab_spec, slab_spec],
        scratch_shapes=[pltpu.VMEM((tc, w2), F32)] * 3 + [pltpu.VMEM((tc, lw), F32)],
        compiler_params=_params(("parallel", "parallel", "arbitrary")),
        name="rwkv_chunk",
    )(z, z, z, z, z, z, z, z, mu2, mu2, mu2, mu2, row(w0), row(a0), row(k_k), row(k_a), row(r_k),
      w_up, a_up, g_up)

    n_pb = n_pairs // pairs_blk
    wblk = pairs_blk * w2
    mat_in = pl.BlockSpec((pairs_blk, 1, w2, w2), lambda bi, pb, ci: (bi * n_pb + pb, ci, 0, 0))
    slab_in = pl.BlockSpec((pairs_blk, ch, w2), lambda bi, pb, ci: (pb, bi * n_chunks + ci, 0))
    par_in = pl.BlockSpec((pairs_blk, 1, w2), lambda bi, pb, ci: (pb, 0, 0))
    by_pair = lambda v_: v_.reshape(n_pairs, 1, w2)
    return pl.pallas_call(
        functools.partial(_rwkv_state_kernel, n_pairs_blk=pairs_blk, n_head=n_head),
        out_shape=jax.ShapeDtypeStruct((t, c), BF16),
        grid=(b, n_pb, n_chunks),
        in_specs=[mat_in, mat_in, slab_in, slab_in, slab_in, slab_in, par_in, par_in],
        out_specs=pl.BlockSpec((ch, wblk), lambda bi, pb, ci: (bi * n_chunks + ci, pb)),
        scratch_shapes=[pltpu.VMEM((pairs_blk, w2, w2), F32)],
        compiler_params=_params(("parallel", "parallel", "arbitrary")),
        name="rwkv_state",
    )(m_mat, g_mat, rp, y0, bonus, gate, by_pair(ln_w), by_pair(ln_b))


def _pitch(d):
    return d // LANES + 1


def _store_vectors(ref, x):
    n, d = x.shape
    pitch = _pitch(d)
    for c in range(d // LANES):
        ref[pl.ds(c, n, stride=pitch), :] = x[:, c * LANES:(c + 1) * LANES]
    ref[pl.ds(pitch - 1, n, stride=pitch), :] = jnp.zeros((n, LANES), x.dtype)


def _load_vectors(ref, n, d):
    pitch = _pitch(d)
    return jnp.concatenate([ref[pl.ds(c, n, stride=pitch), :] for c in range(d // LANES)], axis=1)


def _router_kernel(x_ref, g_ref, w_ref, b_ref, h_ref, logit_ref, *, eps):
    x = x_ref[...]
    ms = jnp.mean(x * x, axis=-1, keepdims=True)
    h = x * lax.rsqrt(ms + eps) * g_ref[...]
    _store_vectors(h_ref, h)
    logit_ref[...] = _dot_x3(h, w_ref[...]) + b_ref[...]


def _router(x, g, w_router, b_router, tm=256):
    t, d = x.shape
    n = w_router.shape[1]
    pitch = _pitch(d)
    return pl.pallas_call(
        functools.partial(_router_kernel, eps=RMS_EPS),
        out_shape=[jax.ShapeDtypeStruct((t * pitch, LANES), F32), jax.ShapeDtypeStruct((t, n), F32)],
        grid=(t // tm,),
        in_specs=[pl.BlockSpec((tm, d), lambda i: (i, 0)), pl.BlockSpec((1, d), lambda i: (0, 0)),
                  pl.BlockSpec((d, n), lambda i: (0, 0)), pl.BlockSpec((1, n), lambda i: (0, 0))],
        out_specs=[pl.BlockSpec((tm * pitch, LANES), lambda i: (i, 0)), pl.BlockSpec((tm, n), lambda i: (i, 0))],
        compiler_params=_params(("parallel",)),
        name="moe_router",
    )(x, g.reshape(1, d), w_router, b_router)


def _gather_rows(idx_ref, base, src_hbm, dst_vmem, sem, n_rows, d):
    pitch, rows = _pitch(d), d // LANES

    def issue(r, _):
        pltpu.make_async_copy(src_hbm.at[pl.ds(idx_ref[base + r] * pitch, rows), :],
                              dst_vmem.at[pl.ds(r * pitch, rows), :], sem).start()
        return 0
    lax.fori_loop(0, n_rows, issue, 0, unroll=DMA_ISSUE_UNROLL)


def _wait_rows(src_hbm, dst_vmem, sem, n_rows, d):
    total = n_rows * (d // LANES)
    pltpu.make_async_copy(src_hbm.at[pl.ds(0, total), :], dst_vmem.at[pl.ds(0, total), :], sem).wait()


def _dispatch_kernel(tok_ref, n_used_ref, h_hbm, o_ref, buf, sem, *, blk):
    i = pl.program_id(0)
    n_used = n_used_ref[0]
    d = o_ref.shape[1]

    def issue(step):
        slot = step & 1
        _gather_rows(tok_ref, step * blk, h_hbm, buf.at[slot], sem.at[slot], blk, d)

    @pl.when((i == 0) & (n_used > 0))
    def _():
        issue(i)

    @pl.when(i + 1 < n_used)
    def _():
        issue(i + 1)

    @pl.when(i < n_used)
    def _():
        slot = i & 1
        _wait_rows(h_hbm, buf.at[slot], sem.at[slot], blk, d)
        o_ref[...] = _load_vectors(buf.at[slot], blk, d).astype(o_ref.dtype)

    @pl.when(i >= n_used)
    def _():
        o_ref[...] = jnp.zeros_like(o_ref)


def _dispatch(slot_tok, n_used, h, d, blk):
    n_slots = slot_tok.shape[0]
    return pl.pallas_call(
        functools.partial(_dispatch_kernel, blk=blk),
        out_shape=jax.ShapeDtypeStruct((n_slots, d), BF16),
        grid_spec=pltpu.PrefetchScalarGridSpec(
            num_scalar_prefetch=2,
            grid=(n_slots // blk,),
            in_specs=[pl.BlockSpec(memory_space=pl.ANY)],
            out_specs=pl.BlockSpec((blk, d), lambda i, tok, nu: (i, 0)),
            scratch_shapes=[pltpu.VMEM((2, blk * _pitch(d), LANES), F32), pltpu.SemaphoreType.DMA((2,))],
        ),
        compiler_params=_params(("arbitrary",)),
        name="moe_dispatch",
    )(slot_tok, n_used, h)


def _expert_up_kernel(blk_e_ref, n_used_ref, x_ref, wg_ref, wu_ref, o_ref):
    used = pl.program_id(1) < n_used_ref[0]

    @pl.when(used)
    def _():
        x = x_ref[...]
        gate = _dot(x, wg_ref[...].astype(BF16))
        up = _dot(x, wu_ref[...].astype(BF16))
        o_ref[...] = (gate * _sigmoid(gate) * up).astype(o_ref.dtype)

    @pl.when(jnp.logical_not(used))
    def _():
        o_ref[...] = jnp.zeros_like(o_ref)


def _expert_down_kernel(blk_e_ref, n_used_ref, h_ref, wd_ref, o_ref):
    used = pl.program_id(0) < n_used_ref[0]

    @pl.when(used)
    def _():
        _store_vectors(o_ref, _dot(h_ref[...], wd_ref[...].astype(BF16)))

    @pl.when(jnp.logical_not(used))
    def _():
        o_ref[...] = jnp.zeros_like(o_ref)


def _experts(blk_e, n_used, xs, wg, wu, wd, layer, blk, tn=256):
    n_slots, d = xs.shape
    de = wg.shape[-1]
    n_blk = n_slots // blk
    tn = min(tn, de)
    used = lambda i, nu: jnp.minimum(i, nu[0] - 1)
    w_up_spec = pl.BlockSpec((None, None, d, tn), lambda j, i, be, nu: (layer, be[used(i, nu)], 0, j))
    hid = pl.pallas_call(
        _expert_up_kernel,
        out_shape=jax.ShapeDtypeStruct((n_slots, de), BF16),
        grid_spec=pltpu.PrefetchScalarGridSpec(
            num_scalar_prefetch=2,
            grid=(de // tn, n_blk),
            in_specs=[pl.BlockSpec((blk, d), lambda j, i, be, nu: (used(i, nu), 0)), w_up_spec, w_up_spec],
            out_specs=pl.BlockSpec((blk, tn), lambda j, i, be, nu: (i, j)),
        ),
        compiler_params=_params(("parallel", "arbitrary")),
        name="moe_expert_up",
    )(blk_e, n_used, xs, wg, wu)
    pitch = _pitch(d)
    return pl.pallas_call(
        _expert_down_kernel,
        out_shape=jax.ShapeDtypeStruct((n_slots * pitch, LANES), F32),
        grid_spec=pltpu.PrefetchScalarGridSpec(
            num_scalar_prefetch=2,
            grid=(n_blk,),
            in_specs=[pl.BlockSpec((blk, de), lambda i, be, nu: (used(i, nu), 0)),
                      pl.BlockSpec((None, None, de, d), lambda i, be, nu: (layer, be[used(i, nu)], 0, 0))],
            out_specs=pl.BlockSpec((blk * pitch, LANES), lambda i, be, nu: (i, 0)),
        ),
        compiler_params=_params(("arbitrary",)),
        name="moe_expert_down",
    )(blk_e, n_used, hid, wd)


def _combine_kernel(pos0_ref, pos1_ref, x_ref, w_ref, g_ref, y_hbm, *refs, tb, n_steps, emit_x, eps):
    outs, (buf0, buf1, sem0, sem1) = refs[:-4], refs[-4:]
    i = pl.program_id(0)
    d = x_ref.shape[1]

    def issue(step):
        slot = step & 1
        _gather_rows(pos0_ref, step * tb, y_hbm, buf0.at[slot], sem0.at[slot], tb, d)
        _gather_rows(pos1_ref, step * tb, y_hbm, buf1.at[slot], sem1.at[slot], tb, d)

    @pl.when(i == 0)
    def _():
        issue(i)

    @pl.when(i + 1 < n_steps)
    def _():
        issue(i + 1)

    slot = i & 1
    w = w_ref[...]
    _wait_rows(y_hbm, buf0.at[slot], sem0.at[slot], tb, d)
    _wait_rows(y_hbm, buf1.at[slot], sem1.at[slot], tb, d)
    y0 = _load_vectors(buf0.at[slot], tb, d)
    y1 = _load_vectors(buf1.at[slot], tb, d)
    x = x_ref[...] + (y0 * w[:, 0:1] + y1 * w[:, 1:2])
    if emit_x:
        outs[0][...] = x
    ms = jnp.mean(x * x, axis=-1, keepdims=True)
    outs[-1][...] = (x * lax.rsqrt(ms + eps) * g_ref[...]).astype(outs[-1].dtype)


def _combine(pos0, pos1, x, wts, y, gain, norm_dtype, emit_x, tb=256):
    t, d = x.shape
    row_spec = pl.BlockSpec((tb, d), lambda i, p0, p1: (i, 0))
    out_shape = [jax.ShapeDtypeStruct((t, d), norm_dtype)]
    if emit_x:
        out_shape.insert(0, jax.ShapeDtypeStruct((t, d), F32))
    return pl.pallas_call(
        functools.partial(_combine_kernel, tb=tb, n_steps=t // tb, emit_x=emit_x, eps=RMS_EPS),
        out_shape=out_shape,
        grid_spec=pltpu.PrefetchScalarGridSpec(
            num_scalar_prefetch=2,
            grid=(t // tb,),
            in_specs=[row_spec,
                      pl.BlockSpec((tb, TOP_K), lambda i, p0, p1: (i, 0)),
                      pl.BlockSpec((1, d), lambda i, p0, p1: (0, 0)),
                      pl.BlockSpec(memory_space=pl.ANY)],
            out_specs=[row_spec] * len(out_shape),
            scratch_shapes=[pltpu.VMEM((2, tb * _pitch(d), LANES), F32), pltpu.VMEM((2, tb * _pitch(d), LANES), F32),
                            pltpu.SemaphoreType.DMA((2,)), pltpu.SemaphoreType.DMA((2,))],
        ),
        compiler_params=_params(("arbitrary",)),
        name="moe_combine",
    )(pos0, pos1, x, wts, gain.reshape(1, d), y)


def _route(logits, n_groups, n_experts, blk):
    t = logits.shape[0]
    epg = n_experts // n_groups
    g_logits = logits[:, :n_groups]
    p_group = jax.nn.softmax(g_logits, axis=-1)
    g_idx = jnp.argmax(g_logits, axis=-1)
    p_sel = jnp.take_along_axis(p_group, g_idx[:, None], axis=1)
    e_logits = logits[:, n_groups:n_groups + n_experts].reshape(t, n_groups, epg)
    e_sel = jnp.take_along_axis(e_logits, g_idx[:, None, None], axis=1)[:, 0]
    top_v, top_i = lax.top_k(e_sel, TOP_K)
    wts = jax.nn.softmax(top_v, axis=-1) * p_sel
    eid = (g_idx[:, None] * epg + top_i).astype(jnp.int32)
    e_flat = eid.reshape(-1)
    n_assign = e_flat.shape[0]
    onehot = (e_flat[:, None] == jnp.arange(n_experts, dtype=jnp.int32)[None, :]).astype(jnp.int32)
    csum = jnp.cumsum(onehot, axis=0)
    counts = csum[-1]
    rank = jnp.take_along_axis(csum, e_flat[:, None], axis=1)[:, 0] - 1
    padded = (counts + blk - 1) // blk * blk
    pends = jnp.cumsum(padded)
    pstarts = pends - padded
    pos = (pstarts[e_flat] + rank).astype(jnp.int32)
    n_blk = (n_assign + blk - 1) // blk + n_experts
    tok_flat = jnp.repeat(jnp.arange(t, dtype=jnp.int32), TOP_K)
    slot_tok = jnp.zeros((n_blk * blk,), jnp.int32).at[pos].set(tok_flat)
    blk_start = jnp.arange(n_blk, dtype=jnp.int32) * blk
    blk_e = jnp.minimum(jnp.sum(blk_start[:, None] >= pends[None, :], axis=1), n_experts - 1).astype(jnp.int32)
    pos2 = pos.reshape(t, TOP_K)
    n_used = (pends[-1:] // blk).astype(jnp.int32)
    return slot_tok, blk_e, n_used, pos2[:, 0], pos2[:, 1], wts.astype(F32)


def _hier_moe(x, ffn_norm, w_rg, b_rg, w_re, b_re, wg, wu, wd, layer, next_gain, next_dtype, emit_x):
    n_groups, n_experts = w_rg.shape[1], w_re.shape[1]
    n_r = n_groups + n_experts
    n_pad = -n_r % LANES
    w_router = jnp.pad(jnp.concatenate([w_rg, w_re], axis=1), ((0, 0), (0, n_pad)))
    b_router = jnp.pad(jnp.concatenate([b_rg, b_re], axis=0), (0, n_pad)).reshape(1, -1)
    h, logits = _router(x, ffn_norm, w_router, b_router)
    slot_tok, blk_e, n_used, pos0, pos1, wts = _route(logits, n_groups, n_experts, MOE_BLK)
    xs = _dispatch(slot_tok, n_used, h, x.shape[1], MOE_BLK)
    y = _experts(blk_e, n_used, xs, wg, wu, wd, layer, MOE_BLK)
    return _combine(pos0, pos1, x, wts, y, next_gain, next_dtype, emit_x)


def kernel(x, attn_norm, w_in, diff_lambda, diff_head_norm, rwkv_mu, rwkv_w0, rwkv_w_up, rwkv_a0, rwkv_a_up, rwkv_g_up, rwkv_k_k, rwkv_k_a, rwkv_r_k, rwkv_ln_w, rwkv_ln_b, w_out, ffn_norm, router_group, router_group_bias, router_expert, router_expert_bias, expert_w_gate, expert_w_up, expert_w_down, final_norm):
    b, s, d = x.shape
    depth = w_in.shape[0]
    t = b * s
    dv = diff_head_norm.shape[-1]
    rw = rwkv_w0.shape[-1]
    dw = d - rw
    n_heads = dw // dv
    attn_cols = 3 * dw
    rwkv_cols = w_in.shape[-1] - attn_cols
    slopes = jnp.asarray([LOG2E * 2.0 ** (-8.0 * (i + 1) / n_heads) for i in range(n_heads)], F32)
    tm = min(1024, t)
    tm_in = min(2048, t)

    xt = x.reshape(t, d)
    h = _rmsnorm(xt, attn_norm[0], BF16)
    for l in range(depth):
        lam_init = 0.8 - 0.6 * math.exp(-0.3 * l)
        last = l == depth - 1
        proj_a = _matmul([h], w_in, l, 0, attn_cols, BF16, tm_in, _tile(attn_cols, (512, 256, 128)),
                         name="in_proj_attn")
        proj_r = _matmul([h], w_in, l, attn_cols, rwkv_cols, F32, tm_in, 256, name="in_proj_rwkv")
        y_diff = _diff_attention(proj_a, diff_lambda[l], diff_head_norm[l], slopes, b, s, n_heads, lam_init)
        y_rwkv = _rwkv7(proj_r, rwkv_mu[l], rwkv_w0[l], rwkv_w_up[l], rwkv_a0[l], rwkv_a_up[l], rwkv_g_up[l],
                        rwkv_k_k[l], rwkv_k_a[l], rwkv_r_k[l], rwkv_ln_w[l], rwkv_ln_b[l], b, s)
        xt = _matmul([y_diff, y_rwkv], w_out, l, 0, d, F32, tm, _tile(d, (512, 256, 128)), res=xt, name="out_proj")
        outs = _hier_moe(xt, ffn_norm[l], router_group[l], router_group_bias[l], router_expert[l],
                         router_expert_bias[l], expert_w_gate, expert_w_up, expert_w_down, l,
                         final_norm if last else attn_norm[l + 1], F32 if last else BF16, not last)
        if not last:
            xt, h = outs
    return outs[0].reshape(b, s, d)
```

```python
import functools
import math

import jax
import jax.numpy as jnp
from jax import lax
from jax.experimental import pallas as pl
from jax.experimental.pallas import tpu as pltpu

F32 = jnp.float32
BF16 = jnp.bfloat16

LANES = 128
VMEM_BUDGET_BYTES = 56 << 20

MASK_CHUNK = 64
RMS_EPS = 1e-6
HEAD_NORM_EPS = 1e-5
GN_EPS = 64e-5
DECAY_SCALE = math.exp(-0.5)
TOP_K = 2
LORA_W = 64
LORA_A = 64
LORA_G = 128

ATTN_BLOCK = 512
LOG2E = math.log2(math.e)
RWKV_CHUNK = 64
RWKV_GROUP = 8
RWKV_BLOCK = 2048
MOE_BLK = 512
DMA_ISSUE_UNROLL = 8


def _tile(n, candidates):
    return next(c for c in candidates if n % c == 0)


def _params(semantics):
    return pltpu.CompilerParams(dimension_semantics=semantics, vmem_limit_bytes=VMEM_BUDGET_BYTES)


def _dot(a, b):
    return jnp.dot(a, b, preferred_element_type=F32)


def _dot_nt(a, b):
    return lax.dot_general(a, b, (((1,), (1,)), ((), ())), preferred_element_type=F32)


def _dot_tn(a, b):
    return lax.dot_general(a, b, (((0,), (0,)), ((), ())), preferred_element_type=F32)


def _split2(x):
    hi = x.astype(BF16)
    lo = (x - hi.astype(F32)).astype(BF16)
    return hi, lo


def _split3(x):
    hi = x.astype(BF16)
    r1 = x - hi.astype(F32)
    mid = r1.astype(BF16)
    lo = (r1 - mid.astype(F32)).astype(BF16)
    return hi, mid, lo


def _dot_x3(a, b):
    ah, al = _split2(a)
    bh, bl = _split2(b)
    return _dot(ah, bh) + _dot(ah, bl) + _dot(al, bh)


def _iota(shape, dim):
    return lax.broadcasted_iota(jnp.int32, shape, dim)


def _sigmoid(x):
    return 1.0 / (1.0 + jnp.exp(-x))


def _rmsnorm_kernel(x_ref, g_ref, o_ref, *, eps):
    x = x_ref[...]
    ms = jnp.mean(x * x, axis=-1, keepdims=True)
    o_ref[...] = (x * lax.rsqrt(ms + eps) * g_ref[...]).astype(o_ref.dtype)


def _rmsnorm(x, g, out_dtype, tm=256):
    t, d = x.shape
    return pl.pallas_call(
        functools.partial(_rmsnorm_kernel, eps=RMS_EPS),
        out_shape=jax.ShapeDtypeStruct((t, d), out_dtype),
        grid=(t // tm,),
        in_specs=[pl.BlockSpec((tm, d), lambda i: (i, 0)), pl.BlockSpec((1, d), lambda i: (0, 0))],
        out_specs=pl.BlockSpec((tm, d), lambda i: (i, 0)),
        compiler_params=_params(("parallel",)),
        name="rmsnorm",
    )(x, g.reshape(1, d))


def _mm_kernel(*refs, n_a, has_res):
    a_refs, w_refs = refs[:n_a], refs[n_a:2 * n_a]
    o_ref = refs[-1]
    acc = None
    for a_ref, w_ref in zip(a_refs, w_refs):
        d = _dot(a_ref[...], w_ref[...].astype(BF16))
        acc = d if acc is None else acc + d
    if has_res:
        acc = acc + refs[2 * n_a][...]
    o_ref[...] = acc.astype(o_ref.dtype)


def _matmul(a_list, w, layer, col_off, n_out, out_dtype, tm, tn, res=None, name="matmul"):
    m, ka = a_list[0].shape
    n_a = len(a_list)
    assert w.shape[1] == ka * n_a and col_off % tn == 0 and n_out % tn == 0 and m % tm == 0
    in_specs = [pl.BlockSpec((tm, ka), lambda i, j: (i, 0)) for _ in a_list]
    in_specs += [pl.BlockSpec((None, ka, tn),
                              functools.partial(lambda i, j, r, c: (layer, r, j + c), r=r, c=col_off // tn))
                 for r in range(n_a)]
    args = list(a_list) + [w] * n_a
    if res is not None:
        in_specs.append(pl.BlockSpec((tm, tn), lambda i, j: (i, j)))
        args.append(res)
    return pl.pallas_call(
        functools.partial(_mm_kernel, n_a=n_a, has_res=res is not None),
        out_shape=jax.ShapeDtypeStruct((m, n_out), out_dtype),
        grid=(m // tm, n_out // tn),
        in_specs=in_specs,
        out_specs=pl.BlockSpec((tm, tn), lambda i, j: (i, j)),
        compiler_params=_params(("parallel", "arbitrary")),
        name=name,
    )(*args)


def _attn_kernel(slopes_ref, lam_ref, diag_ref, q_ref, k_ref, v_ref, hn_ref, o_ref, *, tq, tk, dk, lam_init):
    h = pl.program_id(1)
    qi = pl.program_id(2)
    slope = slopes_ref[h]
    q0 = qi * tq
    dv = q_ref.shape[1]

    lam_p = lam_ref[...]
    lam = (jnp.exp(jnp.sum(lam_p[0:1] * lam_p[1:2], keepdims=True))
           - jnp.exp(jnp.sum(lam_p[2:3] * lam_p[3:4], keepdims=True)) + lam_init)

    q = q_ref[...] * (dk ** -0.5 * LOG2E)
    lane = _iota((tq, dv), 1)
    zero = jnp.zeros_like(q)
    q2 = jnp.concatenate([jnp.where(lane < dk, q, zero), jnp.where(lane >= dk, q, zero)], axis=0)

    def blk(i):
        return pl.multiple_of(i * tk, tk)

    def scores(k0):
        return _dot_nt(q2, k_ref[pl.ds(k0, tk), :])

    def update(s, k0, masked, carry):
        v = v_ref[pl.ds(k0, tk), :]
        m, l, acc = carry
        if masked:
            bias = slope * diag_ref[...]
            s = s + jnp.concatenate([bias, bias], axis=0)
        else:
            s = s + slope * (k0 - q0 + _iota((1, tk), 1)).astype(F32)
        m_new = jnp.maximum(m, jnp.max(s, axis=-1, keepdims=True))
        alpha = jnp.exp2(m - m_new)
        p = jnp.exp2(s - m_new)
        l = alpha * l + jnp.sum(p, axis=-1, keepdims=True)
        acc = alpha * acc + _dot(p.astype(BF16), v)
        return m_new, l, acc

    carry = (jnp.full((2 * tq, 1), -jnp.inf, F32), jnp.zeros((2 * tq, 1), F32), jnp.zeros((2 * tq, dv), F32))
    def pair(j, c):
        ka, kb = blk(2 * j), blk(2 * j + 1)
        sa, sb = scores(ka), scores(kb)
        return update(sb, kb, False, update(sa, ka, False, c))

    def tail_odd(c):
        ka, kb = blk(qi - 1), blk(qi)
        sa, sb = scores(ka), scores(kb)
        return update(sb, kb, True, update(sa, ka, False, c))

    def tail_even(c):
        return update(scores(blk(qi)), blk(qi), True, c)

    carry = lax.fori_loop(0, jnp.right_shift(qi, 1), pair, carry)
    carry = lax.cond((qi & 1) == 1, tail_odd, tail_even, carry)
    _, l, acc = carry
    o = acc / l
    o = o[:tq] - lam * o[tq:]
    ms = jnp.mean(o * o, axis=-1, keepdims=True)
    o = o * lax.rsqrt(ms + HEAD_NORM_EPS) * hn_ref[...]
    o_ref[...] = (o * (1.0 - lam_init)).astype(o_ref.dtype)


def _diff_attention(proj, lam_p, head_norm, slopes, b, s, n_heads, lam_init):
    t = proj.shape[0]
    dv = head_norm.shape[-1]
    dk = lam_p.shape[-1]
    tq = tk = min(ATTN_BLOCK, s)
    assert s % tq == 0 and tq % MASK_CHUNK == 0
    nq = s // tq
    r, c = jnp.arange(tq)[:, None], jnp.arange(tk)[None, :]
    diag = jnp.where(c // MASK_CHUNK <= r // MASK_CHUNK, (r - jnp.abs(r - c)).astype(F32), -jnp.inf)
    return pl.pallas_call(
        functools.partial(_attn_kernel, tq=tq, tk=tk, dk=dk, lam_init=lam_init),
        out_shape=jax.ShapeDtypeStruct((t, n_heads * dv), BF16),
        grid=(b, n_heads, nq),
        in_specs=[
            pl.BlockSpec(memory_space=pltpu.SMEM),
            pl.BlockSpec(lam_p.shape, lambda bi, h, qi: (0, 0)),
            pl.BlockSpec((tq, tk), lambda bi, h, qi: (0, 0)),
            pl.BlockSpec((tq, dv), lambda bi, h, qi: (bi * nq + qi, h)),
            pl.BlockSpec((s, dv), lambda bi, h, qi: (bi, n_heads + h)),
            pl.BlockSpec((s, dv), lambda bi, h, qi: (bi, 2 * n_heads + h)),
            pl.BlockSpec((1, dv), lambda bi, h, qi: (0, 0)),
        ],
        out_specs=pl.BlockSpec((tq, dv), lambda bi, h, qi: (bi * nq + qi, h)),
        compiler_params=_params(("parallel", "parallel", "arbitrary")),
        name="diff_attention",
    )(slopes, lam_p, diag, proj, proj, proj, head_norm.reshape(1, dv))


def _rwkv_chunk_kernel(zr_ref, zk_ref, zv_ref, zl_ref, hr_ref, hk_ref, hv_ref, hl_ref,
                       mur_ref, muk_ref, muv_ref, mul_ref, w0_ref, a0_ref, kk_ref, ka_ref, rk_ref,
                       wup_ref, aup_ref, gup_ref,
                       m_ref, g_ref, rp_ref, y0_ref, bonus_ref, gate_ref,
                       sr, sk, sv, sl, *, tc, n_head):
    c = RWKV_CHUNK
    n_chunks = tc // c
    first = pl.program_id(2) == 0
    w2 = 2 * n_head
    pt = 2 * c

    def shifted(z_ref, h_ref, mu_ref, out_ref):
        z = z_ref[...]
        prev_row = jnp.where(first, 0.0, h_ref[7:8, :])
        zp = pltpu.roll(z, 1, axis=0)
        zp = jnp.where(_iota(z.shape, 0) == 0, prev_row, zp)
        out_ref[...] = z + mu_ref[...] * (zp - z)

    shifted(zr_ref, hr_ref, mur_ref, sr)
    shifted(zk_ref, hk_ref, muk_ref, sk)
    shifted(zv_ref, hv_ref, muv_ref, sv)
    shifted(zl_ref, hl_ref, mul_ref, sl)

    grp = min(RWKV_GROUP, n_chunks)
    n = grp * c
    m0 = _iota((c, w2), 1) < n_head
    m0n = _iota((n, w2), 1) < n_head
    row_p, col_p = _iota((pt, pt), 0), _iota((pt, pt), 1)
    same_head = (row_p // c) == (col_p // c)
    strict = same_head & ((row_p % c) > (col_p % c))
    incl = same_head & ((row_p % c) >= (col_p % c))
    eye_p = (row_p == col_p).astype(F32)
    row_l, col_l = _iota((w2, w2), 0), _iota((w2, w2), 1)
    head_blk = (row_l // n_head) == (col_l // n_head)
    seg_ones = head_blk.astype(BF16)
    tri = (_iota((grp, c, c), 1) >= _iota((grp, c, c), 2)).astype(BF16)
    w_cat = jnp.concatenate([wup_ref[...], aup_ref[...]], axis=0)
    g_up = gup_ref[...].astype(BF16)

    def seg_sum(x):
        hi, lo = _split2(x)
        return _dot(hi, seg_ones) + _dot(lo, seg_ones)

    def bmm(a, b):
        return lax.dot_general(a, b, (((2,), (1,)), ((0,), (0,))), preferred_element_type=F32)

    def bmm_nt(a, b):
        return lax.dot_general(a, b, (((2,), (2,)), ((0,), (0,))), preferred_element_type=F32)

    def bmm_tn(a, b):
        return lax.dot_general(a, b, (((1,), (1,)), ((0,), (0,))), preferred_element_type=F32)

    def by_chunk(x):
        return x.reshape(grp, c, x.shape[-1])

    def stack(x):
        z = jnp.zeros_like(x)
        return jnp.concatenate([jnp.where(m0, x, z), jnp.where(m0, z, x)], axis=1)

    def unstack(x):
        return x[:, :c] + x[:, c:]

    PREP, SOLVE = 0, 1

    def chunk_group(gi):
        rows = slice(gi * n, (gi + 1) * n)
        r, k, v, lora = sr[rows, :], sk[rows, :], sv[rows, :], sl[rows, :]
        d_wa = lora[:, :LORA_W + LORA_A]
        zero = jnp.zeros_like(d_wa)
        logit_w = w0_ref[...] + _dot_x3(jnp.where(m0n, jnp.tanh(d_wa), zero), w_cat)
        logit_a = a0_ref[...] + _dot_x3(jnp.where(m0n, zero, d_wa), w_cat)
        yield PREP
        logw = -DECAY_SCALE * _sigmoid(logit_w)
        a = _sigmoid(logit_a)
        kk = k * kk_ref[...]
        kap = kk / jnp.maximum(jnp.sqrt(seg_sum(kk * kk)), 1e-12)
        yield PREP
        kh = k * (1.0 + (a - 1.0) * ka_ref[...])
        bvec = kap * a
        lw_h, lw_m, lw_l = _split3(logw)
        cum3 = bmm(tri, by_chunk(lw_h)) + bmm(tri, by_chunk(lw_m)) + bmm(tri, by_chunk(lw_l))
        cum = cum3.reshape(n, w2)
        yield PREP
        cum_end = cum3[:, c - 1:c, :]
        inv_dec = jnp.exp(-cum)
        kap_t = stack(by_chunk(kap * jnp.exp(cum - logw)))
        yield PREP
        r_t = stack(by_chunk(r * jnp.exp(cum)))
        lhs = jnp.concatenate([kap_t, r_t], axis=1).astype(BF16)
        yield PREP
        rhs = jnp.concatenate([stack(by_chunk(bvec * inv_dec)), stack(by_chunk(kh * inv_dec))],
                              axis=1).astype(BF16)
        yield PREP
        v3 = by_chunk(v)
        v_s = stack(v3).astype(BF16)
        dec_out = jnp.exp(cum_end - cum3)
        b_out = (by_chunk(bvec) * dec_out).astype(BF16)
        k_out = (by_chunk(kh) * dec_out).astype(BF16)
        yield PREP
        gate_ref[rows, :] = _dot(_sigmoid(lora[:, LORA_W + LORA_A:]).astype(BF16), g_up)
        bonus_ref[rows, :] = seg_sum(r * kh * rk_ref[...]) * v
        yield PREP

        aa = bmm_nt(lhs, rhs)
        kv = bmm_tn(k_out, v3.astype(BF16))
        yield SOLVE
        a_kb = jnp.where(strict, aa[:, :pt, :pt], 0.0)
        a_kk = jnp.where(strict, aa[:, :pt, pt:], 0.0)
        a_rb = jnp.where(incl, aa[:, pt:, :pt], 0.0).astype(BF16)
        a_rk = jnp.where(incl, aa[:, pt:, pt:], 0.0).astype(BF16)

        x = a_kb
        tinv = eye_p - x
        for _ in range(int(math.log2(c)) - 1):
            xb = x.astype(BF16)
            x = bmm(xb, xb)
            yield SOLVE
            tinv = bmm(tinv.astype(BF16), (eye_p + x).astype(BF16))
            yield SOLVE

        akv = bmm(a_kk.astype(BF16), v_s)
        yield SOLVE
        pq = bmm(tinv.astype(BF16), jnp.concatenate([kap_t, akv], axis=2).astype(BF16))
        yield SOLVE
        arb_pq = bmm(a_rb, pq.astype(BF16))
        rp = r_t - arb_pq[:, :, :w2]
        yield SOLVE
        y0 = bmm(a_rk, v_s) - arb_pq[:, :, w2:]
        pq_s = unstack(pq).astype(BF16)
        yield SOLVE
        bt_pq = bmm_tn(b_out, pq_s)
        m_mat = eye_p[:w2, :w2] * jnp.exp(cum_end) - jnp.where(head_blk, bt_pq[:, :, :w2], 0.0)
        g_mat = jnp.where(head_blk, kv - bt_pq[:, :, w2:], 0.0)

        chunks = slice(gi * grp, (gi + 1) * grp)
        m_ref[chunks] = m_mat
        g_ref[chunks] = g_mat
        rp_ref[rows, :] = unstack(rp).reshape(n, w2)
        y0_ref[rows, :] = unstack(y0).reshape(n, w2)
        yield SOLVE

    def run_prep(gen, max_stages):
        for _ in range(max_stages):
            if next(gen) == SOLVE:
                return False
        return True

    groups = [chunk_group(gi) for gi in range(n_chunks // grp)]
    run_prep(groups[0], 1 << 30)
    for gi, gen in enumerate(groups):
        nxt = groups[gi + 1] if gi + 1 < len(groups) else None
        nxt_in_prep = nxt is not None
        for _ in gen:
            if nxt_in_prep:
                nxt_in_prep = run_prep(nxt, 1)
        if nxt_in_prep:
            run_prep(nxt, 1 << 30)


def _rwkv_state_kernel(m_ref, g_ref, rp_ref, y0_ref, bonus_ref, gate_ref, lnw_ref, lnb_ref, o_ref, st_ref,
                       *, n_pairs_blk, n_head):
    @pl.when(pl.program_id(2) == 0)
    def _():
        st_ref[...] = jnp.zeros_like(st_ref)

    w2 = 2 * n_head
    head_blk = ((_iota((w2, w2), 0) // n_head) == (_iota((w2, w2), 1) // n_head)).astype(BF16)

    def seg_mean(x):
        hi, lo = _split2(x)
        return (_dot(hi, head_blk) + _dot(lo, head_blk)) * (1.0 / n_head)

    def bmm(a, b):
        return lax.dot_general(a, b, (((2,), (1,)), ((0,), (0,))), preferred_element_type=F32)

    ch = rp_ref.shape[1]
    st_hi, st_lo = _split2(st_ref[...])
    rp = rp_ref[...].astype(BF16)
    y = bmm(rp, st_hi) + bmm(rp, st_lo) + y0_ref[...]
    m = m_ref[:, 0].astype(BF16)
    st_ref[...] = bmm(m, st_hi) + bmm(m, st_lo) + g_ref[:, 0]
    y = y.reshape(n_pairs_blk * ch, w2)
    mean = seg_mean(y)
    yc = y - mean
    var = seg_mean(yc * yc)
    yn = (yc * lax.rsqrt(var + GN_EPS)).reshape(n_pairs_blk, ch, w2)
    out = (yn * lnw_ref[...] + lnb_ref[...] + bonus_ref[...]) * gate_ref[...]
    for p in range(n_pairs_blk):
        o_ref[:, p * w2:(p + 1) * w2] = out[p].astype(o_ref.dtype)


def _rwkv7(z, mu, w0, w_up, a0, a_up, g_up, k_k, k_a, r_k, ln_w, ln_b, b, s, tc=RWKV_BLOCK, pairs_blk=16):
    t = z.shape[0]
    c = w0.shape[-1]
    n_head = r_k.shape[-1]
    w2 = 2 * n_head
    assert w2 == LANES and (3 * c) % (2 * LANES) == 0 and LORA_W + LORA_A == LANES
    n_pairs = c // w2
    tc = min(tc, s)
    pairs_blk = min(pairs_blk, n_pairs)
    n_tb = s // tc
    ch = RWKV_CHUNK
    n_chunks = s // ch
    cpb = tc // ch
    lw = LORA_W + LORA_A + LORA_G
    cb = c // w2
    lb = 3 * c // lw
    mu2 = mu.reshape(1, -1)
    row = lambda v_: v_.reshape(1, c)

    def zspec(col_blk_fn, width):
        return pl.BlockSpec((tc, width), lambda bi, p, ti: (bi * n_tb + ti, col_blk_fn(p)))

    def hspec(col_blk_fn, width):
        return pl.BlockSpec((8, width),
                            lambda bi, p, ti: (jnp.maximum((bi * n_tb + ti) * (tc // 8) - 1, 0), col_blk_fn(p)))

    def pspec(col_blk_fn, width, rows=1):
        return pl.BlockSpec((rows, width), lambda bi, p, ti: (0, col_blk_fn(p)))

    sec = [lambda p: p, lambda p: cb + p, lambda p: 2 * cb + p]
    in_specs = ([zspec(f, w2) for f in sec] + [zspec(lambda p: lb, lw)]
                + [hspec(f, w2) for f in sec] + [hspec(lambda p: lb, lw)]
                + [pspec(f, w2) for f in sec] + [pspec(lambda p: lb, lw)]
                + [pspec(sec[0], w2) for _ in range(5)]
                + [pspec(sec[0], w2, LORA_W), pspec(sec[0], w2, LORA_A), pspec(sec[0], w2, LORA_G)])
    mat_shape = jax.ShapeDtypeStruct((b * n_pairs, n_chunks, w2, w2), F32)
    slab_shape = jax.ShapeDtypeStruct((n_pairs, t, w2), F32)
    mat_spec = pl.BlockSpec((None, cpb, w2, w2), lambda bi, p, ti: (bi * n_pairs + p, ti, 0, 0))
    slab_spec = pl.BlockSpec((None, tc, w2), lambda bi, p, ti: (p, bi * n_tb + ti, 0))
    m_mat, g_mat, rp, y0, bonus, gate = pl.pallas_call(
        functools.partial(_rwkv_chunk_kernel, tc=tc, n_head=n_head),
        out_shape=[mat_shape, mat_shape, slab_shape, slab_shape, slab_shape, slab_shape],
        grid=(b, n_pairs, n_tb),
        in_specs=in_specs,
        out_specs=[mat_spec, mat_spec, slab_spec, slab_spec, slab_spec, slab_spec],
        scratch_shapes=[pltpu.VMEM((tc, w2), F32)] * 3 + [pltpu.VMEM((tc, lw), F32)],
        compiler_params=_params(("parallel", "parallel", "arbitrary")),
        name="rwkv_chunk",
    )(z, z, z, z, z, z, z, z, mu2, mu2, mu2, mu2, row(w0), row(a0), row(k_k), row(k_a), row(r_k),
      w_up, a_up, g_up)

    n_pb = n_pairs // pairs_blk
    wblk = pairs_blk * w2
    mat_in = pl.BlockSpec((pairs_blk, 1, w2, w2), lambda bi, pb, ci: (bi * n_pb + pb, ci, 0, 0))
    slab_in = pl.BlockSpec((pairs_blk, ch, w2), lambda bi, pb, ci: (pb, bi * n_chunks + ci, 0))
    par_in = pl.BlockSpec((pairs_blk, 1, w2), lambda bi, pb, ci: (pb, 0, 0))
    by_pair = lambda v_: v_.reshape(n_pairs, 1, w2)
    return pl.pallas_call(
        functools.partial(_rwkv_state_kernel, n_pairs_blk=pairs_blk, n_head=n_head),
        out_shape=jax.ShapeDtypeStruct((t, c), BF16),
        grid=(b, n_pb, n_chunks),
        in_specs=[mat_in, mat_in, slab_in, slab_in, slab_in, slab_in, par_in, par_in],
        out_specs=pl.BlockSpec((ch, wblk), lambda bi, pb, ci: (bi * n_chunks + ci, pb)),
        scratch_shapes=[pltpu.VMEM((pairs_blk, w2, w2), F32)],
        compiler_params=_params(("parallel", "parallel", "arbitrary")),
        name="rwkv_state",
    )(m_mat, g_mat, rp, y0, bonus, gate, by_pair(ln_w), by_pair(ln_b))


def _pitch(d):
    return d // LANES + 1


def _store_vectors(ref, x):
    n, d = x.shape
    pitch = _pitch(d)
    for c in range(d // LANES):
        ref[pl.ds(c, n, stride=pitch), :] = x[:, c * LANES:(c + 1) * LANES]
    ref[pl.ds(pitch - 1, n, stride=pitch), :] = jnp.zeros((n, LANES), x.dtype)


def _load_vectors(ref, n, d):
    pitch = _pitch(d)
    return jnp.concatenate([ref[pl.ds(c, n, stride=pitch), :] for c in range(d // LANES)], axis=1)


def _router_kernel(x_ref, g_ref, w_ref, b_ref, h_ref, logit_ref, *, eps):
    x = x_ref[...]
    ms = jnp.mean(x * x, axis=-1, keepdims=True)
    h = x * lax.rsqrt(ms + eps) * g_ref[...]
    _store_vectors(h_ref, h)
    logit_ref[...] = _dot_x3(h, w_ref[...]) + b_ref[...]


def _router(x, g, w_router, b_router, tm=256):
    t, d = x.shape
    n = w_router.shape[1]
    pitch = _pitch(d)
    return pl.pallas_call(
        functools.partial(_router_kernel, eps=RMS_EPS),
        out_shape=[jax.ShapeDtypeStruct((t * pitch, LANES), F32), jax.ShapeDtypeStruct((t, n), F32)],
        grid=(t // tm,),
        in_specs=[pl.BlockSpec((tm, d), lambda i: (i, 0)), pl.BlockSpec((1, d), lambda i: (0, 0)),
                  pl.BlockSpec((d, n), lambda i: (0, 0)), pl.BlockSpec((1, n), lambda i: (0, 0))],
        out_specs=[pl.BlockSpec((tm * pitch, LANES), lambda i: (i, 0)), pl.BlockSpec((tm, n), lambda i: (i, 0))],
        compiler_params=_params(("parallel",)),
        name="moe_router",
    )(x, g.reshape(1, d), w_router, b_router)


def _gather_rows(idx_ref, base, src_hbm, dst_vmem, sem, n_rows, d):
    pitch, rows = _pitch(d), d // LANES

    def issue(r, _):
        pltpu.make_async_copy(src_hbm.at[pl.ds(idx_ref[base + r] * pitch, rows), :],
                              dst_vmem.at[pl.ds(r * pitch, rows), :], sem).start()
        return 0
    lax.fori_loop(0, n_rows, issue, 0, unroll=DMA_ISSUE_UNROLL)


def _wait_rows(src_hbm, dst_vmem, sem, n_rows, d):
    total = n_rows * (d // LANES)
    pltpu.make_async_copy(src_hbm.at[pl.ds(0, total), :], dst_vmem.at[pl.ds(0, total), :], sem).wait()


def _dispatch_kernel(tok_ref, n_used_ref, h_hbm, o_ref, buf, sem, *, blk):
    i = pl.program_id(0)
    n_used = n_used_ref[0]
    d = o_ref.shape[1]

    def issue(step):
        slot = step & 1
        _gather_rows(tok_ref, step * blk, h_hbm, buf.at[slot], sem.at[slot], blk, d)

    @pl.when((i == 0) & (n_used > 0))
    def _():
        issue(i)

    @pl.when(i + 1 < n_used)
    def _():
        issue(i + 1)

    @pl.when(i < n_used)
    def _():
        slot = i & 1
        _wait_rows(h_hbm, buf.at[slot], sem.at[slot], blk, d)
        o_ref[...] = _load_vectors(buf.at[slot], blk, d).astype(o_ref.dtype)

    @pl.when(i >= n_used)
    def _():
        o_ref[...] = jnp.zeros_like(o_ref)


def _dispatch(slot_tok, n_used, h, d, blk):
    n_slots = slot_tok.shape[0]
    return pl.pallas_call(
        functools.partial(_dispatch_kernel, blk=blk),
        out_shape=jax.ShapeDtypeStruct((n_slots, d), BF16),
        grid_spec=pltpu.PrefetchScalarGridSpec(
            num_scalar_prefetch=2,
            grid=(n_slots // blk,),
            in_specs=[pl.BlockSpec(memory_space=pl.ANY)],
            out_specs=pl.BlockSpec((blk, d), lambda i, tok, nu: (i, 0)),
            scratch_shapes=[pltpu.VMEM((2, blk * _pitch(d), LANES), F32), pltpu.SemaphoreType.DMA((2,))],
        ),
        compiler_params=_params(("arbitrary",)),
        name="moe_dispatch",
    )(slot_tok, n_used, h)


def _expert_up_kernel(blk_e_ref, n_used_ref, x_ref, wg_ref, wu_ref, o_ref):
    used = pl.program_id(1) < n_used_ref[0]

    @pl.when(used)
    def _():
        x = x_ref[...]
        gate = _dot(x, wg_ref[...].astype(BF16))
        up = _dot(x, wu_ref[...].astype(BF16))
        o_ref[...] = (gate * _sigmoid(gate) * up).astype(o_ref.dtype)

    @pl.when(jnp.logical_not(used))
    def _():
        o_ref[...] = jnp.zeros_like(o_ref)


def _expert_down_kernel(blk_e_ref, n_used_ref, h_ref, wd_ref, o_ref):
    used = pl.program_id(0) < n_used_ref[0]

    @pl.when(used)
    def _():
        _store_vectors(o_ref, _dot(h_ref[...], wd_ref[...].astype(BF16)))

    @pl.when(jnp.logical_not(used))
    def _():
        o_ref[...] = jnp.zeros_like(o_ref)


def _experts(blk_e, n_used, xs, wg, wu, wd, layer, blk, tn=256):
    n_slots, d = xs.shape
    de = wg.shape[-1]
    n_blk = n_slots // blk
    tn = min(tn, de)
    used = lambda i, nu: jnp.minimum(i, nu[0] - 1)
    w_up_spec = pl.BlockSpec((None, None, d, tn), lambda j, i, be, nu: (layer, be[used(i, nu)], 0, j))
    hid = pl.pallas_call(
        _expert_up_kernel,
        out_shape=jax.ShapeDtypeStruct((n_slots, de), BF16),
        grid_spec=pltpu.PrefetchScalarGridSpec(
            num_scalar_prefetch=2,
            grid=(de // tn, n_blk),
            in_specs=[pl.BlockSpec((blk, d), lambda j, i, be, nu: (used(i, nu), 0)), w_up_spec, w_up_spec],
            out_specs=pl.BlockSpec((blk, tn), lambda j, i, be, nu: (i, j)),
        ),
        compiler_params=_params(("parallel", "arbitrary")),
        name="moe_expert_up",
    )(blk_e, n_used, xs, wg, wu)
    pitch = _pitch(d)
    return pl.pallas_call(
        _expert_down_kernel,
        out_shape=jax.ShapeDtypeStruct((n_slots * pitch, LANES), F32),
        grid_spec=pltpu.PrefetchScalarGridSpec(
            num_scalar_prefetch=2,
            grid=(n_blk,),
            in_specs=[pl.BlockSpec((blk, de), lambda i, be, nu: (used(i, nu), 0)),
                      pl.BlockSpec((None, None, de, d), lambda i, be, nu: (layer, be[used(i, nu)], 0, 0))],
            out_specs=pl.BlockSpec((blk * pitch, LANES), lambda i, be, nu: (i, 0)),
        ),
        compiler_params=_params(("arbitrary",)),
        name="moe_expert_down",
    )(blk_e, n_used, hid, wd)


def _combine_kernel(pos0_ref, pos1_ref, x_ref, w_ref, g_ref, y_hbm, *refs, tb, n_steps, emit_x, eps):
    outs, (buf0, buf1, sem0, sem1) = refs[:-4], refs[-4:]
    i = pl.program_id(0)
    d = x_ref.shape[1]

    def issue(step):
        slot = step & 1
        _gather_rows(pos0_ref, step * tb, y_hbm, buf0.at[slot], sem0.at[slot], tb, d)
        _gather_rows(pos1_ref, step * tb, y_hbm, buf1.at[slot], sem1.at[slot], tb, d)

    @pl.when(i == 0)
    def _():
        issue(i)

    @pl.when(i + 1 < n_steps)
    def _():
        issue(i + 1)

    slot = i & 1
    w = w_ref[...]
    _wait_rows(y_hbm, buf0.at[slot], sem0.at[slot], tb, d)
    _wait_rows(y_hbm, buf1.at[slot], sem1.at[slot], tb, d)
    y0 = _load_vectors(buf0.at[slot], tb, d)
    y1 = _load_vectors(buf1.at[slot], tb, d)
    x = x_ref[...] + (y0 * w[:, 0:1] + y1 * w[:, 1:2])
    if emit_x:
        outs[0][...] = x
    ms = jnp.mean(x * x, axis=-1, keepdims=True)
    outs[-1][...] = (x * lax.rsqrt(ms + eps) * g_ref[...]).astype(outs[-1].dtype)


def _combine(pos0, pos1, x, wts, y, gain, norm_dtype, emit_x, tb=256):
    t, d = x.shape
    row_spec = pl.BlockSpec((tb, d), lambda i, p0, p1: (i, 0))
    out_shape = [jax.ShapeDtypeStruct((t, d), norm_dtype)]
    if emit_x:
        out_shape.insert(0, jax.ShapeDtypeStruct((t, d), F32))
    return pl.pallas_call(
        functools.partial(_combine_kernel, tb=tb, n_steps=t // tb, emit_x=emit_x, eps=RMS_EPS),
        out_shape=out_shape,
        grid_spec=pltpu.PrefetchScalarGridSpec(
            num_scalar_prefetch=2,
            grid=(t // tb,),
            in_specs=[row_spec,
                      pl.BlockSpec((tb, TOP_K), lambda i, p0, p1: (i, 0)),
                      pl.BlockSpec((1, d), lambda i, p0, p1: (0, 0)),
                      pl.BlockSpec(memory_space=pl.ANY)],
            out_specs=[row_spec] * len(out_shape),
            scratch_shapes=[pltpu.VMEM((2, tb * _pitch(d), LANES), F32), pltpu.VMEM((2, tb * _pitch(d), LANES), F32),
                            pltpu.SemaphoreType.DMA((2,)), pltpu.SemaphoreType.DMA((2,))],
        ),
        compiler_params=_params(("arbitrary",)),
        name="moe_combine",
    )(pos0, pos1, x, wts, gain.reshape(1, d), y)


def _route(logits, n_groups, n_experts, blk):
    t = logits.shape[0]
    epg = n_experts // n_groups
    g_logits = logits[:, :n_groups]
    p_group = jax.nn.softmax(g_logits, axis=-1)
    g_idx = jnp.argmax(g_logits, axis=-1)
    p_sel = jnp.take_along_axis(p_group, g_idx[:, None], axis=1)
    e_logits = logits[:, n_groups:n_groups + n_experts].reshape(t, n_groups, epg)
    e_sel = jnp.take_along_axis(e_logits, g_idx[:, None, None], axis=1)[:, 0]
    top_v, top_i = lax.top_k(e_sel, TOP_K)
    wts = jax.nn.softmax(top_v, axis=-1) * p_sel
    eid = (g_idx[:, None] * epg + top_i).astype(jnp.int32)
    e_flat = eid.reshape(-1)
    n_assign = e_flat.shape[0]
    onehot = (e_flat[:, None] == jnp.arange(n_experts, dtype=jnp.int32)[None, :]).astype(jnp.int32)
    csum = jnp.cumsum(onehot, axis=0)
    counts = csum[-1]
    rank = jnp.take_along_axis(csum, e_flat[:, None], axis=1)[:, 0] - 1
    padded = (counts + blk - 1) // blk * blk
    pends = jnp.cumsum(padded)
    pstarts = pends - padded
    pos = (pstarts[e_flat] + rank).astype(jnp.int32)
    n_blk = (n_assign + blk - 1) // blk + n_experts
    tok_flat = jnp.repeat(jnp.arange(t, dtype=jnp.int32), TOP_K)
    slot_tok = jnp.zeros((n_blk * blk,), jnp.int32).at[pos].set(tok_flat)
    blk_start = jnp.arange(n_blk, dtype=jnp.int32) * blk
    blk_e = jnp.minimum(jnp.sum(blk_start[:, None] >= pends[None, :], axis=1), n_experts - 1).astype(jnp.int32)
    pos2 = pos.reshape(t, TOP_K)
    n_used = (pends[-1:] // blk).astype(jnp.int32)
    return slot_tok, blk_e, n_used, pos2[:, 0], pos2[:, 1], wts.astype(F32)


def _hier_moe(x, ffn_norm, w_rg, b_rg, w_re, b_re, wg, wu, wd, layer, next_gain, next_dtype, emit_x):
    n_groups, n_experts = w_rg.shape[1], w_re.shape[1]
    n_r = n_groups + n_experts
    n_pad = -n_r % LANES
    w_router = jnp.pad(jnp.concatenate([w_rg, w_re], axis=1), ((0, 0), (0, n_pad)))
    b_router = jnp.pad(jnp.concatenate([b_rg, b_re], axis=0), (0, n_pad)).reshape(1, -1)
    h, logits = _router(x, ffn_norm, w_router, b_router)
    slot_tok, blk_e, n_used, pos0, pos1, wts = _route(logits, n_groups, n_experts, MOE_BLK)
    xs = _dispatch(slot_tok, n_used, h, x.shape[1], MOE_BLK)
    y = _experts(blk_e, n_used, xs, wg, wu, wd, layer, MOE_BLK)
    return _combine(pos0, pos1, x, wts, y, next_gain, next_dtype, emit_x)


def kernel(x, attn_norm, w_in, diff_lambda, diff_head_norm, rwkv_mu, rwkv_w0, rwkv_w_up, rwkv_a0, rwkv_a_up, rwkv_g_up, rwkv_k_k, rwkv_k_a, rwkv_r_k, rwkv_ln_w, rwkv_ln_b, w_out, ffn_norm, router_group, router_group_bias, router_expert, router_expert_bias, expert_w_gate, expert_w_up, expert_w_down, final_norm):
    b, s, d = x.shape
    depth = w_in.shape[0]
    t = b * s
    dv = diff_head_norm.shape[-1]
    rw = rwkv_w0.shape[-1]
    dw = d - rw
    n_heads = dw // dv
    attn_cols = 3 * dw
    rwkv_cols = w_in.shape[-1] - attn_cols
    slopes = jnp.asarray([LOG2E * 2.0 ** (-8.0 * (i + 1) / n_heads) for i in range(n_heads)], F32)
    tm = min(1024, t)
    tm_in = min(2048, t)

    xt = x.reshape(t, d)
    h = _rmsnorm(xt, attn_norm[0], BF16)
    for l in range(depth):
        lam_init = 0.8 - 0.6 * math.exp(-0.3 * l)
        last = l == depth - 1
        proj_a = _matmul([h], w_in, l, 0, attn_cols, BF16, tm_in, _tile(attn_cols, (512, 256, 128)),
                         name="in_proj_attn")
        proj_r = _matmul([h], w_in, l, attn_cols, rwkv_cols, F32, tm_in, 256, name="in_proj_rwkv")
        y_diff = _diff_attention(proj_a, diff_lambda[l], diff_head_norm[l], slopes, b, s, n_heads, lam_init)
        y_rwkv = _rwkv7(proj_r, rwkv_mu[l], rwkv_w0[l], rwkv_w_up[l], rwkv_a0[l], rwkv_a_up[l], rwkv_g_up[l],
                        rwkv_k_k[l], rwkv_k_a[l], rwkv_r_k[l], rwkv_ln_w[l], rwkv_ln_b[l], b, s)
        xt = _matmul([y_diff, y_rwkv], w_out, l, 0, d, F32, tm, _tile(d, (512, 256, 128)), res=xt, name="out_proj")
        outs = _hier_moe(xt, ffn_norm[l], router_group[l], router_group_bias[l], router_expert[l],
                         router_expert_bias[l], expert_w_gate, expert_w_up, expert_w_down, l,
                         final_norm if last else attn_norm[l + 1], F32 if last else BF16, not last)
        if not last:
            xt, h = outs
    return outs[0].reshape(b, s, d)
```

```python
import functools
import math

import jax
import jax.numpy as jnp
from jax import lax
from jax.experimental import pallas as pl
from jax.experimental.pallas import tpu as pltpu

F32 = jnp.float32
BF16 = jnp.bfloat16

LANES = 128
VMEM_BUDGET_BYTES = 56 << 20

MASK_CHUNK = 64
RMS_EPS = 1e-6
HEAD_NORM_EPS = 1e-5
GN_EPS = 64e-5
DECAY_SCALE = math.exp(-0.5)
TOP_K = 2
LORA_W = 64
LORA_A = 64
LORA_G = 128

ATTN_BLOCK = 512
LOG2E = math.log2(math.e)
RWKV_CHUNK = 64
RWKV_GROUP = 8
RWKV_BLOCK = 2048
MOE_BLK = 512
DISPATCH_BLK = 256
DMA_ISSUE_UNROLL = 8


def _tile(n, candidates):
    return next(c for c in candidates if n % c == 0)


def _params(semantics):
    return pltpu.CompilerParams(dimension_semantics=semantics, vmem_limit_bytes=VMEM_BUDGET_BYTES)


def _dot(a, b):
    return jnp.dot(a, b, preferred_element_type=F32)


def _dot_nt(a, b):
    return lax.dot_general(a, b, (((1,), (1,)), ((), ())), preferred_element_type=F32)


def _dot_tn(a, b):
    return lax.dot_general(a, b, (((0,), (0,)), ((), ())), preferred_element_type=F32)


def _split2(x):
    hi = x.astype(BF16)
    lo = (x - hi.astype(F32)).astype(BF16)
    return hi, lo


def _split3(x):
    hi = x.astype(BF16)
    r1 = x - hi.astype(F32)
    mid = r1.astype(BF16)
    lo = (r1 - mid.astype(F32)).astype(BF16)
    return hi, mid, lo


def _dot_x3(a, b):
    ah, al = _split2(a)
    bh, bl = _split2(b)
    return _dot(ah, bh) + _dot(ah, bl) + _dot(al, bh)


def _iota(shape, dim):
    return lax.broadcasted_iota(jnp.int32, shape, dim)


def _sigmoid(x):
    return 1.0 / (1.0 + jnp.exp(-x))


def _rmsnorm_kernel(x_ref, g_ref, o_ref, *, eps):
    x = x_ref[...]
    ms = jnp.mean(x * x, axis=-1, keepdims=True)
    o_ref[...] = (x * lax.rsqrt(ms + eps) * g_ref[...]).astype(o_ref.dtype)


def _rmsnorm(x, g, out_dtype, tm=256):
    t, d = x.shape
    return pl.pallas_call(
        functools.partial(_rmsnorm_kernel, eps=RMS_EPS),
        out_shape=jax.ShapeDtypeStruct((t, d), out_dtype),
        grid=(t // tm,),
        in_specs=[pl.BlockSpec((tm, d), lambda i: (i, 0)), pl.BlockSpec((1, d), lambda i: (0, 0))],
        out_specs=pl.BlockSpec((tm, d), lambda i: (i, 0)),
        compiler_params=_params(("parallel",)),
        name="rmsnorm",
    )(x, g.reshape(1, d))


def _mm_kernel(*refs, n_a, has_res):
    a_refs, w_refs = refs[:n_a], refs[n_a:2 * n_a]
    o_ref = refs[-1]
    acc = None
    for a_ref, w_ref in zip(a_refs, w_refs):
        d = _dot(a_ref[...], w_ref[...].astype(BF16))
        acc = d if acc is None else acc + d
    if has_res:
        acc = acc + refs[2 * n_a][...]
    o_ref[...] = acc.astype(o_ref.dtype)


def _matmul(a_list, w, layer, col_off, n_out, out_dtype, tm, tn, res=None, name="matmul"):
    m, ka = a_list[0].shape
    n_a = len(a_list)
    assert w.shape[1] == ka * n_a and col_off % tn == 0 and n_out % tn == 0 and m % tm == 0
    in_specs = [pl.BlockSpec((tm, ka), lambda i, j: (i, 0)) for _ in a_list]
    in_specs += [pl.BlockSpec((None, ka, tn),
                              functools.partial(lambda i, j, r, c: (layer, r, j + c), r=r, c=col_off // tn))
                 for r in range(n_a)]
    args = list(a_list) + [w] * n_a
    if res is not None:
        in_specs.append(pl.BlockSpec((tm, tn), lambda i, j: (i, j)))
        args.append(res)
    return pl.pallas_call(
        functools.partial(_mm_kernel, n_a=n_a, has_res=res is not None),
        out_shape=jax.ShapeDtypeStruct((m, n_out), out_dtype),
        grid=(m // tm, n_out // tn),
        in_specs=in_specs,
        out_specs=pl.BlockSpec((tm, tn), lambda i, j: (i, j)),
        compiler_params=_params(("parallel", "arbitrary")),
        name=name,
    )(*args)


def _attn_kernel(slopes_ref, lam_ref, diag_ref, q_ref, k_ref, v_ref, hn_ref, o_ref, *, tq, tk, dk, lam_init):
    h = pl.program_id(1)
    qi = pl.program_id(2)
    slope = slopes_ref[h]
    q0 = qi * tq
    dv = q_ref.shape[1]

    lam_p = lam_ref[...]
    lam = (jnp.exp(jnp.sum(lam_p[0:1] * lam_p[1:2], keepdims=True))
           - jnp.exp(jnp.sum(lam_p[2:3] * lam_p[3:4], keepdims=True)) + lam_init)

    q = q_ref[...] * (dk ** -0.5 * LOG2E)
    lane = _iota((tq, dv), 1)
    zero = jnp.zeros_like(q)
    q2 = jnp.concatenate([jnp.where(lane < dk, q, zero), jnp.where(lane >= dk, q, zero)], axis=0)

    def blk(i):
        return pl.multiple_of(i * tk, tk)

    def scores(k0):
        return _dot_nt(q2, k_ref[pl.ds(k0, tk), :])

    def update(s, k0, masked, carry):
        v = v_ref[pl.ds(k0, tk), :]
        m, l, acc = carry
        if masked:
            bias = slope * diag_ref[...]
            s = s + jnp.concatenate([bias, bias], axis=0)
        else:
            s = s + slope * (k0 - q0 + _iota((1, tk), 1)).astype(F32)
        m_new = jnp.maximum(m, jnp.max(s, axis=-1, keepdims=True))
        alpha = jnp.exp2(m - m_new)
        p = jnp.exp2(s - m_new)
        l = alpha * l + jnp.sum(p, axis=-1, keepdims=True)
        acc = alpha * acc + _dot(p.astype(BF16), v)
        return m_new, l, acc

    carry = (jnp.full((2 * tq, 1), -jnp.inf, F32), jnp.zeros((2 * tq, 1), F32), jnp.zeros((2 * tq, dv), F32))
    def pair(j, c):
        ka, kb = blk(2 * j), blk(2 * j + 1)
        sa, sb = scores(ka), scores(kb)
        return update(sb, kb, False, update(sa, ka, False, c))

    def tail_odd(c):
        ka, kb = blk(qi - 1), blk(qi)
        sa, sb = scores(ka), scores(kb)
        return update(sb, kb, True, update(sa, ka, False, c))

    def tail_even(c):
        return update(scores(blk(qi)), blk(qi), True, c)

    carry = lax.fori_loop(0, jnp.right_shift(qi, 1), pair, carry)
    carry = lax.cond((qi & 1) == 1, tail_odd, tail_even, carry)
    _, l, acc = carry
    o = acc / l
    o = o[:tq] - lam * o[tq:]
    ms = jnp.mean(o * o, axis=-1, keepdims=True)
    o = o * lax.rsqrt(ms + HEAD_NORM_EPS) * hn_ref[...]
    o_ref[...] = (o * (1.0 - lam_init)).astype(o_ref.dtype)


def _diff_attention(proj, lam_p, head_norm, slopes, b, s, n_heads, lam_init):
    t = proj.shape[0]
    dv = head_norm.shape[-1]
    dk = lam_p.shape[-1]
    tq = tk = min(ATTN_BLOCK, s)
    assert s % tq == 0 and tq % MASK_CHUNK == 0
    nq = s // tq
    r, c = jnp.arange(tq)[:, None], jnp.arange(tk)[None, :]
    diag = jnp.where(c // MASK_CHUNK <= r // MASK_CHUNK, (r - jnp.abs(r - c)).astype(F32), -jnp.inf)
    return pl.pallas_call(
        functools.partial(_attn_kernel, tq=tq, tk=tk, dk=dk, lam_init=lam_init),
        out_shape=jax.ShapeDtypeStruct((t, n_heads * dv), BF16),
        grid=(b, n_heads, nq),
        in_specs=[
            pl.BlockSpec(memory_space=pltpu.SMEM),
            pl.BlockSpec(lam_p.shape, lambda bi, h, qi: (0, 0)),
            pl.BlockSpec((tq, tk), lambda bi, h, qi: (0, 0)),
            pl.BlockSpec((tq, dv), lambda bi, h, qi: (bi * nq + qi, h)),
            pl.BlockSpec((s, dv), lambda bi, h, qi: (bi, n_heads + h)),
            pl.BlockSpec((s, dv), lambda bi, h, qi: (bi, 2 * n_heads + h)),
            pl.BlockSpec((1, dv), lambda bi, h, qi: (0, 0)),
        ],
        out_specs=pl.BlockSpec((tq, dv), lambda bi, h, qi: (bi * nq + qi, h)),
        compiler_params=_params(("parallel", "parallel", "arbitrary")),
        name="diff_attention",
    )(slopes, lam_p, diag, proj, proj, proj, head_norm.reshape(1, dv))


def _rwkv_chunk_kernel(zr_ref, zk_ref, zv_ref, zl_ref, hr_ref, hk_ref, hv_ref, hl_ref,
                       mur_ref, muk_ref, muv_ref, mul_ref, w0_ref, a0_ref, kk_ref, ka_ref, rk_ref,
                       wup_ref, aup_ref, gup_ref,
                       m_ref, g_ref, rp_ref, y0_ref, bonus_ref, gate_ref,
                       sr, sk, sv, sl, *, tc, n_head):
    c = RWKV_CHUNK
    n_chunks = tc // c
    first = pl.program_id(2) == 0
    w2 = 2 * n_head
    pt = 2 * c

    def shifted(z_ref, h_ref, mu_ref, out_ref):
        z = z_ref[...]
        prev_row = jnp.where(first, 0.0, h_ref[7:8, :])
        zp = pltpu.roll(z, 1, axis=0)
        zp = jnp.where(_iota(z.shape, 0) == 0, prev_row, zp)
        out_ref[...] = z + mu_ref[...] * (zp - z)

    shifted(zr_ref, hr_ref, mur_ref, sr)
    shifted(zk_ref, hk_ref, muk_ref, sk)
    shifted(zv_ref, hv_ref, muv_ref, sv)
    shifted(zl_ref, hl_ref, mul_ref, sl)

    grp = min(RWKV_GROUP, n_chunks)
    n = grp * c
    m0 = _iota((c, w2), 1) < n_head
    m0n = _iota((n, w2), 1) < n_head
    row_p, col_p = _iota((pt, pt), 0), _iota((pt, pt), 1)
    same_head = (row_p // c) == (col_p // c)
    strict = same_head & ((row_p % c) > (col_p % c))
    incl = same_head & ((row_p % c) >= (col_p % c))
    eye_p = (row_p == col_p).astype(F32)
    row_l, col_l = _iota((w2, w2), 0), _iota((w2, w2), 1)
    head_blk = (row_l // n_head) == (col_l // n_head)
    seg_ones = head_blk.astype(BF16)
    tri = (_iota((grp, c, c), 1) >= _iota((grp, c, c), 2)).astype(BF16)
    w_cat = jnp.concatenate([wup_ref[...], aup_ref[...]], axis=0)
    g_up = gup_ref[...].astype(BF16)

    def seg_sum(x):
        hi, lo = _split2(x)
        return _dot(hi, seg_ones) + _dot(lo, seg_ones)

    def bmm(a, b):
        return lax.dot_general(a, b, (((2,), (1,)), ((0,), (0,))), preferred_element_type=F32)

    def bmm_nt(a, b):
        return lax.dot_general(a, b, (((2,), (2,)), ((0,), (0,))), preferred_element_type=F32)

    def bmm_tn(a, b):
        return lax.dot_general(a, b, (((1,), (1,)), ((0,), (0,))), preferred_element_type=F32)

    def by_chunk(x):
        return x.reshape(grp, c, x.shape[-1])

    def stack(x):
        z = jnp.zeros_like(x)
        return jnp.concatenate([jnp.where(m0, x, z), jnp.where(m0, z, x)], axis=1)

    def unstack(x):
        return x[:, :c] + x[:, c:]

    PREP, SOLVE = 0, 1

    def chunk_group(gi):
        rows = slice(gi * n, (gi + 1) * n)
        r, k, v, lora = sr[rows, :], sk[rows, :], sv[rows, :], sl[rows, :]
        d_wa = lora[:, :LORA_W + LORA_A]
        zero = jnp.zeros_like(d_wa)
        logit_w = w0_ref[...] + _dot_x3(jnp.where(m0n, jnp.tanh(d_wa), zero), w_cat)
        logit_a = a0_ref[...] + _dot_x3(jnp.where(m0n, zero, d_wa), w_cat)
        yield PREP
        logw = -DECAY_SCALE * _sigmoid(logit_w)
        a = _sigmoid(logit_a)
        kk = k * kk_ref[...]
        kap = kk / jnp.maximum(jnp.sqrt(seg_sum(kk * kk)), 1e-12)
        yield PREP
        kh = k * (1.0 + (a - 1.0) * ka_ref[...])
        bvec = kap * a
        lw_h, lw_m, lw_l = _split3(logw)
        cum3 = bmm(tri, by_chunk(lw_h)) + bmm(tri, by_chunk(lw_m)) + bmm(tri, by_chunk(lw_l))
        cum = cum3.reshape(n, w2)
        yield PREP
        cum_end = cum3[:, c - 1:c, :]
        inv_dec = jnp.exp(-cum)
        kap_t = stack(by_chunk(kap * jnp.exp(cum - logw)))
        yield PREP
        r_t = stack(by_chunk(r * jnp.exp(cum)))
        lhs = jnp.concatenate([kap_t, r_t], axis=1).astype(BF16)
        yield PREP
        rhs = jnp.concatenate([stack(by_chunk(bvec * inv_dec)), stack(by_chunk(kh * inv_dec))],
                              axis=1).astype(BF16)
        yield PREP
        v3 = by_chunk(v)
        v_s = stack(v3).astype(BF16)
        dec_out = jnp.exp(cum_end - cum3)
        b_out = (by_chunk(bvec) * dec_out).astype(BF16)
        k_out = (by_chunk(kh) * dec_out).astype(BF16)
        yield PREP
        gate_ref[rows, :] = _dot(_sigmoid(lora[:, LORA_W + LORA_A:]).astype(BF16), g_up)
        bonus_ref[rows, :] = seg_sum(r * kh * rk_ref[...]) * v
        yield PREP

        aa = bmm_nt(lhs, rhs)
        kv = bmm_tn(k_out, v3.astype(BF16))
        yield SOLVE
        a_kb = jnp.where(strict, aa[:, :pt, :pt], 0.0)
        a_kk = jnp.where(strict, aa[:, :pt, pt:], 0.0)
        a_rb = jnp.where(incl, aa[:, pt:, :pt], 0.0).astype(BF16)
        a_rk = jnp.where(incl, aa[:, pt:, pt:], 0.0).astype(BF16)

        x = a_kb
        tinv = eye_p - x
        for _ in range(int(math.log2(c)) - 1):
            xb = x.astype(BF16)
            x = bmm(xb, xb)
            yield SOLVE
            tinv = bmm(tinv.astype(BF16), (eye_p + x).astype(BF16))
            yield SOLVE

        akv = bmm(a_kk.astype(BF16), v_s)
        yield SOLVE
        pq = bmm(tinv.astype(BF16), jnp.concatenate([kap_t, akv], axis=2).astype(BF16))
        yield SOLVE
        arb_pq = bmm(a_rb, pq.astype(BF16))
        rp = r_t - arb_pq[:, :, :w2]
        yield SOLVE
        y0 = bmm(a_rk, v_s) - arb_pq[:, :, w2:]
        pq_s = unstack(pq).astype(BF16)
        yield SOLVE
        bt_pq = bmm_tn(b_out, pq_s)
        m_mat = eye_p[:w2, :w2] * jnp.exp(cum_end) - jnp.where(head_blk, bt_pq[:, :, :w2], 0.0)
        g_mat = jnp.where(head_blk, kv - bt_pq[:, :, w2:], 0.0)

        chunks = slice(gi * grp, (gi + 1) * grp)
        m_ref[chunks] = m_mat
        g_ref[chunks] = g_mat
        rp_ref[rows, :] = unstack(rp).reshape(n, w2)
        y0_ref[rows, :] = unstack(y0).reshape(n, w2)
        yield SOLVE

    def run_prep(gen, max_stages):
        for _ in range(max_stages):
            if next(gen) == SOLVE:
                return False
        return True

    groups = [chunk_group(gi) for gi in range(n_chunks // grp)]
    run_prep(groups[0], 1 << 30)
    for gi, gen in enumerate(groups):
        nxt = groups[gi + 1] if gi + 1 < len(groups) else None
        nxt_in_prep = nxt is not None
        for _ in gen:
            if nxt_in_prep:
                nxt_in_prep = run_prep(nxt, 1)
        if nxt_in_prep:
            run_prep(nxt, 1 << 30)


def _rwkv_state_kernel(m_ref, g_ref, rp_ref, y0_ref, bonus_ref, gate_ref, lnw_ref, lnb_ref, o_ref, st_ref,
                       *, n_pairs_blk, n_head):
    @pl.when(pl.program_id(2) == 0)
    def _():
        st_ref[...] = jnp.zeros_like(st_ref)

    w2 = 2 * n_head
    head_blk = ((_iota((w2, w2), 0) // n_head) == (_iota((w2, w2), 1) // n_head)).astype(BF16)

    def seg_mean(x):
        hi, lo = _split2(x)
        return (_dot(hi, head_blk) + _dot(lo, head_blk)) * (1.0 / n_head)

    def bmm(a, b):
        return lax.dot_general(a, b, (((2,), (1,)), ((0,), (0,))), preferred_element_type=F32)

    ch = rp_ref.shape[1]
    st_hi, st_lo = _split2(st_ref[...])
    rp = rp_ref[...].astype(BF16)
    y = bmm(rp, st_hi) + bmm(rp, st_lo) + y0_ref[...]
    m = m_ref[:, 0].astype(BF16)
    st_ref[...] = bmm(m, st_hi) + bmm(m, st_lo) + g_ref[:, 0]
    y = y.reshape(n_pairs_blk * ch, w2)
    mean = seg_mean(y)
    yc = y - mean
    var = seg_mean(yc * yc)
    yn = (yc * lax.rsqrt(var + GN_EPS)).reshape(n_pairs_blk, ch, w2)
    out = (yn * lnw_ref[...] + lnb_ref[...] + bonus_ref[...]) * gate_ref[...]
    for p in range(n_pairs_blk):
        o_ref[:, p * w2:(p + 1) * w2] = out[p].astype(o_ref.dtype)


def _rwkv7(z, mu, w0, w_up, a0, a_up, g_up, k_k, k_a, r_k, ln_w, ln_b, b, s, tc=RWKV_BLOCK, pairs_blk=16):
    t = z.shape[0]
    c = w0.shape[-1]
    n_head = r_k.shape[-1]
    w2 = 2 * n_head
    assert w2 == LANES and (3 * c) % (2 * LANES) == 0 and LORA_W + LORA_A == LANES
    n_pairs = c // w2
    tc = min(tc, s)
    pairs_blk = min(pairs_blk, n_pairs)
    n_tb = s // tc
    ch = RWKV_CHUNK
    n_chunks = s // ch
    cpb = tc // ch
    lw = LORA_W + LORA_A + LORA_G
    cb = c // w2
    lb = 3 * c // lw
    mu2 = mu.reshape(1, -1)
    row = lambda v_: v_.reshape(1, c)

    def zspec(col_blk_fn, width):
        return pl.BlockSpec((tc, width), lambda bi, p, ti: (bi * n_tb + ti, col_blk_fn(p)))

    def hspec(col_blk_fn, width):
        return pl.BlockSpec((8, width),
                            lambda bi, p, ti: (jnp.maximum((bi * n_tb + ti) * (tc // 8) - 1, 0), col_blk_fn(p)))

    def pspec(col_blk_fn, width, rows=1):
        return pl.BlockSpec((rows, width), lambda bi, p, ti: (0, col_blk_fn(p)))

    sec = [lambda p: p, lambda p: cb + p, lambda p: 2 * cb + p]
    in_specs = ([zspec(f, w2) for f in sec] + [zspec(lambda p: lb, lw)]
                + [hspec(f, w2) for f in sec] + [hspec(lambda p: lb, lw)]
                + [pspec(f, w2) for f in sec] + [pspec(lambda p: lb, lw)]
                + [pspec(sec[0], w2) for _ in range(5)]
                + [pspec(sec[0], w2, LORA_W), pspec(sec[0], w2, LORA_A), pspec(sec[0], w2, LORA_G)])
    mat_shape = jax.ShapeDtypeStruct((b * n_pairs, n_chunks, w2, w2), F32)
    slab_shape = jax.ShapeDtypeStruct((n_pairs, t, w2), F32)
    mat_spec = pl.BlockSpec((None, cpb, w2, w2), lambda bi, p, ti: (bi * n_pairs + p, ti, 0, 0))
    slab_spec = pl.BlockSpec((None, tc, w2), lambda bi, p, ti: (p, bi * n_tb + ti, 0))
    m_mat, g_mat, rp, y0, bonus, gate = pl.pallas_call(
        functools.partial(_rwkv_chunk_kernel, tc=tc, n_head=n_head),
        out_shape=[mat_shape, mat_shape, slab_shape, slab_shape, slab_shape, slab_shape],
        grid=(b, n_pairs, n_tb),
        in_specs=in_specs,
        out_specs=[mat_spec, mat_spec, slab_spec, slab_spec, slab_spec, slab_spec],
        scratch_shapes=[pltpu.VMEM((tc, w2), F32)] * 3 + [pltpu.VMEM((tc, lw), F32)],
        compiler_params=_params(("parallel", "parallel", "arbitrary")),
        name="rwkv_chunk",
    )(z, z, z, z, z, z, z, z, mu2, mu2, mu2, mu2, row(w0), row(a0), row(k_k), row(k_a), row(r_k),
      w_up, a_up, g_up)

    n_pb = n_pairs // pairs_blk
    wblk = pairs_blk * w2
    mat_in = pl.BlockSpec((pairs_blk, 1, w2, w2), lambda bi, pb, ci: (bi * n_pb + pb, ci, 0, 0))
    slab_in = pl.BlockSpec((pairs_blk, ch, w2), lambda bi, pb, ci: (pb, bi * n_chunks + ci, 0))
    par_in = pl.BlockSpec((pairs_blk, 1, w2), lambda bi, pb, ci: (pb, 0, 0))
    by_pair = lambda v_: v_.reshape(n_pairs, 1, w2)
    return pl.pallas_call(
        functools.partial(_rwkv_state_kernel, n_pairs_blk=pairs_blk, n_head=n_head),
        out_shape=jax.ShapeDtypeStruct((t, c), BF16),
        grid=(b, n_pb, n_chunks),
        in_specs=[mat_in, mat_in, slab_in, slab_in, slab_in, slab_in, par_in, par_in],
        out_specs=pl.BlockSpec((ch, wblk), lambda bi, pb, ci: (bi * n_chunks + ci, pb)),
        scratch_shapes=[pltpu.VMEM((pairs_blk, w2, w2), F32)],
        compiler_params=_params(("parallel", "parallel", "arbitrary")),
        name="rwkv_state",
    )(m_mat, g_mat, rp, y0, bonus, gate, by_pair(ln_w), by_pair(ln_b))


def _pitch(d):
    return d // LANES + 1


def _store_vectors(ref, x):
    n, d = x.shape
    pitch = _pitch(d)
    for c in range(d // LANES):
        ref[pl.ds(c, n, stride=pitch), :] = x[:, c * LANES:(c + 1) * LANES]
    ref[pl.ds(pitch - 1, n, stride=pitch), :] = jnp.zeros((n, LANES), x.dtype)


def _load_vectors(ref, n, d):
    pitch = _pitch(d)
    return jnp.concatenate([ref[pl.ds(c, n, stride=pitch), :] for c in range(d // LANES)], axis=1)


def _router_kernel(x_ref, g_ref, w_ref, b_ref, h_ref, logit_ref, *, eps):
    x = x_ref[...]
    ms = jnp.mean(x * x, axis=-1, keepdims=True)
    h = x * lax.rsqrt(ms + eps) * g_ref[...]
    _store_vectors(h_ref, h)
    logit_ref[...] = _dot_x3(h, w_ref[...]) + b_ref[...]


def _router(x, g, w_router, b_router, tm=256):
    t, d = x.shape
    n = w_router.shape[1]
    pitch = _pitch(d)
    return pl.pallas_call(
        functools.partial(_router_kernel, eps=RMS_EPS),
        out_shape=[jax.ShapeDtypeStruct((t * pitch, LANES), F32), jax.ShapeDtypeStruct((t, n), F32)],
        grid=(t // tm,),
        in_specs=[pl.BlockSpec((tm, d), lambda i: (i, 0)), pl.BlockSpec((1, d), lambda i: (0, 0)),
                  pl.BlockSpec((d, n), lambda i: (0, 0)), pl.BlockSpec((1, n), lambda i: (0, 0))],
        out_specs=[pl.BlockSpec((tm * pitch, LANES), lambda i: (i, 0)), pl.BlockSpec((tm, n), lambda i: (i, 0))],
        compiler_params=_params(("parallel",)),
        name="moe_router",
    )(x, g.reshape(1, d), w_router, b_router)


def _gather_rows(idx_ref, base, src_hbm, dst_vmem, sem, n_rows, d):
    pitch, rows = _pitch(d), d // LANES

    def issue(r, _):
        pltpu.make_async_copy(src_hbm.at[pl.ds(idx_ref[base + r] * pitch, rows), :],
                              dst_vmem.at[pl.ds(r * pitch, rows), :], sem).start()
        return 0
    lax.fori_loop(0, n_rows, issue, 0, unroll=DMA_ISSUE_UNROLL)


def _wait_rows(src_hbm, dst_vmem, sem, n_rows, d):
    total = n_rows * (d // LANES)
    pltpu.make_async_copy(src_hbm.at[pl.ds(0, total), :], dst_vmem.at[pl.ds(0, total), :], sem).wait()


def _dispatch_kernel(tok_ref, n_used_ref, h_hbm, o_ref, buf, sem, *, blk):
    i = pl.program_id(0)
    n_used = n_used_ref[0]
    d = o_ref.shape[1]

    def issue(step):
        slot = step & 1
        _gather_rows(tok_ref, step * blk, h_hbm, buf.at[slot], sem.at[slot], blk, d)

    @pl.when((i == 0) & (n_used > 0))
    def _():
        issue(i)

    @pl.when(i + 1 < n_used)
    def _():
        issue(i + 1)

    @pl.when(i < n_used)
    def _():
        slot = i & 1
        _wait_rows(h_hbm, buf.at[slot], sem.at[slot], blk, d)
        o_ref[...] = _load_vectors(buf.at[slot], blk, d).astype(o_ref.dtype)

    @pl.when(i >= n_used)
    def _():
        o_ref[...] = jnp.zeros_like(o_ref)


def _dispatch(slot_tok, n_used, h, d, blk):
    n_slots = slot_tok.shape[0]
    return pl.pallas_call(
        functools.partial(_dispatch_kernel, blk=blk),
        out_shape=jax.ShapeDtypeStruct((n_slots, d), BF16),
        grid_spec=pltpu.PrefetchScalarGridSpec(
            num_scalar_prefetch=2,
            grid=(n_slots // blk,),
            in_specs=[pl.BlockSpec(memory_space=pl.ANY)],
            out_specs=pl.BlockSpec((blk, d), lambda i, tok, nu: (i, 0)),
            scratch_shapes=[pltpu.VMEM((2, blk * _pitch(d), LANES), F32), pltpu.SemaphoreType.DMA((2,))],
        ),
        compiler_params=_params(("arbitrary",)),
        name="moe_dispatch",
    )(slot_tok, n_used, h)


def _expert_up_kernel(blk_e_ref, n_used_ref, x_ref, wg_ref, wu_ref, o_ref):
    used = pl.program_id(1) < n_used_ref[0]

    @pl.when(used)
    def _():
        x = x_ref[...]
        gate = _dot(x, wg_ref[...].astype(BF16))
        up = _dot(x, wu_ref[...].astype(BF16))
        o_ref[...] = (gate * _sigmoid(gate) * up).astype(o_ref.dtype)

    @pl.when(jnp.logical_not(used))
    def _():
        o_ref[...] = jnp.zeros_like(o_ref)


def _expert_down_kernel(blk_e_ref, n_used_ref, h_ref, wd_ref, o_ref):
    used = pl.program_id(0) < n_used_ref[0]

    @pl.when(used)
    def _():
        _store_vectors(o_ref, _dot(h_ref[...], wd_ref[...].astype(BF16)))

    @pl.when(jnp.logical_not(used))
    def _():
        o_ref[...] = jnp.zeros_like(o_ref)


def _experts(blk_e, n_used, xs, wg, wu, wd, layer, blk, tn=256):
    n_slots, d = xs.shape
    de = wg.shape[-1]
    n_blk = n_slots // blk
    tn = min(tn, de)
    used = lambda i, nu: jnp.minimum(i, nu[0] - 1)
    w_up_spec = pl.BlockSpec((None, None, d, tn), lambda j, i, be, nu: (layer, be[used(i, nu)], 0, j))
    hid = pl.pallas_call(
        _expert_up_kernel,
        out_shape=jax.ShapeDtypeStruct((n_slots, de), BF16),
        grid_spec=pltpu.PrefetchScalarGridSpec(
            num_scalar_prefetch=2,
            grid=(de // tn, n_blk),
            in_specs=[pl.BlockSpec((blk, d), lambda j, i, be, nu: (used(i, nu), 0)), w_up_spec, w_up_spec],
            out_specs=pl.BlockSpec((blk, tn), lambda j, i, be, nu: (i, j)),
        ),
        compiler_params=_params(("parallel", "arbitrary")),
        name="moe_expert_up",
    )(blk_e, n_used, xs, wg, wu)
    pitch = _pitch(d)
    return pl.pallas_call(
        _expert_down_kernel,
        out_shape=jax.ShapeDtypeStruct((n_slots * pitch, LANES), F32),
        grid_spec=pltpu.PrefetchScalarGridSpec(
            num_scalar_prefetch=2,
            grid=(n_blk,),
            in_specs=[pl.BlockSpec((blk, de), lambda i, be, nu: (used(i, nu), 0)),
                      pl.BlockSpec((None, None, de, d), lambda i, be, nu: (layer, be[used(i, nu)], 0, 0))],
            out_specs=pl.BlockSpec((blk * pitch, LANES), lambda i, be, nu: (i, 0)),
        ),
        compiler_params=_params(("arbitrary",)),
        name="moe_expert_down",
    )(blk_e, n_used, hid, wd)


def _combine_kernel(pos0_ref, pos1_ref, x_ref, w_ref, g_ref, y_hbm, *refs, tb, n_steps, emit_x, eps):
    outs, (buf0, buf1, sem0, sem1) = refs[:-4], refs[-4:]
    i = pl.program_id(0)
    d = x_ref.shape[1]

    def issue(step):
        slot = step & 1
        _gather_rows(pos0_ref, step * tb, y_hbm, buf0.at[slot], sem0.at[slot], tb, d)
        _gather_rows(pos1_ref, step * tb, y_hbm, buf1.at[slot], sem1.at[slot], tb, d)

    @pl.when(i == 0)
    def _():
        issue(i)

    @pl.when(i + 1 < n_steps)
    def _():
        issue(i + 1)

    slot = i & 1
    w = w_ref[...]
    _wait_rows(y_hbm, buf0.at[slot], sem0.at[slot], tb, d)
    _wait_rows(y_hbm, buf1.at[slot], sem1.at[slot], tb, d)
    y0 = _load_vectors(buf0.at[slot], tb, d)
    y1 = _load_vectors(buf1.at[slot], tb, d)
    x = x_ref[...] + (y0 * w[:, 0:1] + y1 * w[:, 1:2])
    if emit_x:
        outs[0][...] = x
    ms = jnp.mean(x * x, axis=-1, keepdims=True)
    outs[-1][...] = (x * lax.rsqrt(ms + eps) * g_ref[...]).astype(outs[-1].dtype)


def _combine(pos0, pos1, x, wts, y, gain, norm_dtype, emit_x, tb=256):
    t, d = x.shape
    row_spec = pl.BlockSpec((tb, d), lambda i, p0, p1: (i, 0))
    out_shape = [jax.ShapeDtypeStruct((t, d), norm_dtype)]
    if emit_x:
        out_shape.insert(0, jax.ShapeDtypeStruct((t, d), F32))
    return pl.pallas_call(
        functools.partial(_combine_kernel, tb=tb, n_steps=t // tb, emit_x=emit_x, eps=RMS_EPS),
        out_shape=out_shape,
        grid_spec=pltpu.PrefetchScalarGridSpec(
            num_scalar_prefetch=2,
            grid=(t // tb,),
            in_specs=[row_spec,
                      pl.BlockSpec((tb, TOP_K), lambda i, p0, p1: (i, 0)),
                      pl.BlockSpec((1, d), lambda i, p0, p1: (0, 0)),
                      pl.BlockSpec(memory_space=pl.ANY)],
            out_specs=[row_spec] * len(out_shape),
            scratch_shapes=[pltpu.VMEM((2, tb * _pitch(d), LANES), F32), pltpu.VMEM((2, tb * _pitch(d), LANES), F32),
                            pltpu.SemaphoreType.DMA((2,)), pltpu.SemaphoreType.DMA((2,))],
        ),
        compiler_params=_params(("arbitrary",)),
        name="moe_combine",
    )(pos0, pos1, x, wts, gain.reshape(1, d), y)


def _route(logits, n_groups, n_experts, blk):
    t = logits.shape[0]
    epg = n_experts // n_groups
    g_logits = logits[:, :n_groups]
    p_group = jax.nn.softmax(g_logits, axis=-1)
    g_idx = jnp.argmax(g_logits, axis=-1)
    p_sel = jnp.take_along_axis(p_group, g_idx[:, None], axis=1)
    e_logits = logits[:, n_groups:n_groups + n_experts].reshape(t, n_groups, epg)
    e_sel = jnp.take_along_axis(e_logits, g_idx[:, None, None], axis=1)[:, 0]
    top_v, top_i = lax.top_k(e_sel, TOP_K)
    wts = jax.nn.softmax(top_v, axis=-1) * p_sel
    eid = (g_idx[:, None] * epg + top_i).astype(jnp.int32)
    e_flat = eid.reshape(-1)
    n_assign = e_flat.shape[0]
    onehot = (e_flat[:, None] == jnp.arange(n_experts, dtype=jnp.int32)[None, :]).astype(jnp.int32)
    csum = jnp.cumsum(onehot, axis=0)
    counts = csum[-1]
    rank = jnp.take_along_axis(csum, e_flat[:, None], axis=1)[:, 0] - 1
    padded = (counts + blk - 1) // blk * blk
    pends = jnp.cumsum(padded)
    pstarts = pends - padded
    pos = (pstarts[e_flat] + rank).astype(jnp.int32)
    n_blk = (n_assign + blk - 1) // blk + n_experts
    tok_flat = jnp.repeat(jnp.arange(t, dtype=jnp.int32), TOP_K)
    slot_tok = jnp.zeros((n_blk * blk,), jnp.int32).at[pos].set(tok_flat)
    blk_start = jnp.arange(n_blk, dtype=jnp.int32) * blk
    blk_e = jnp.minimum(jnp.sum(blk_start[:, None] >= pends[None, :], axis=1), n_experts - 1).astype(jnp.int32)
    pos2 = pos.reshape(t, TOP_K)
    n_used = (pends[-1:] // blk).astype(jnp.int32)
    return slot_tok, blk_e, n_used, pos2[:, 0], pos2[:, 1], wts.astype(F32)


def _hier_moe(x, ffn_norm, w_rg, b_rg, w_re, b_re, wg, wu, wd, layer, next_gain, next_dtype, emit_x):
    n_groups, n_experts = w_rg.shape[1], w_re.shape[1]
    n_r = n_groups + n_experts
    n_pad = -n_r % LANES
    w_router = jnp.pad(jnp.concatenate([w_rg, w_re], axis=1), ((0, 0), (0, n_pad)))
    b_router = jnp.pad(jnp.concatenate([b_rg, b_re], axis=0), (0, n_pad)).reshape(1, -1)
    h, logits = _router(x, ffn_norm, w_router, b_router)
    slot_tok, blk_e, n_used, pos0, pos1, wts = _route(logits, n_groups, n_experts, MOE_BLK)
    xs = _dispatch(slot_tok, n_used * (MOE_BLK // DISPATCH_BLK), h, x.shape[1], DISPATCH_BLK)
    y = _experts(blk_e, n_used, xs, wg, wu, wd, layer, MOE_BLK)
    return _combine(pos0, pos1, x, wts, y, next_gain, next_dtype, emit_x)


def kernel(x, attn_norm, w_in, diff_lambda, diff_head_norm, rwkv_mu, rwkv_w0, rwkv_w_up, rwkv_a0, rwkv_a_up, rwkv_g_up, rwkv_k_k, rwkv_k_a, rwkv_r_k, rwkv_ln_w, rwkv_ln_b, w_out, ffn_norm, router_group, router_group_bias, router_expert, router_expert_bias, expert_w_gate, expert_w_up, expert_w_down, final_norm):
    b, s, d = x.shape
    depth = w_in.shape[0]
    t = b * s
    dv = diff_head_norm.shape[-1]
    rw = rwkv_w0.shape[-1]
    dw = d - rw
    n_heads = dw // dv
    attn_cols = 3 * dw
    rwkv_cols = w_in.shape[-1] - attn_cols
    slopes = jnp.asarray([LOG2E * 2.0 ** (-8.0 * (i + 1) / n_heads) for i in range(n_heads)], F32)
    tm = min(1024, t)
    tm_in = min(2048, t)

    xt = x.reshape(t, d)
    h = _rmsnorm(xt, attn_norm[0], BF16)
    for l in range(depth):
        lam_init = 0.8 - 0.6 * math.exp(-0.3 * l)
        last = l == depth - 1
        proj_a = _matmul([h], w_in, l, 0, attn_cols, BF16, tm_in, _tile(attn_cols, (512, 256, 128)),
                         name="in_proj_attn")
        proj_r = _matmul([h], w_in, l, attn_cols, rwkv_cols, F32, tm_in, 256, name="in_proj_rwkv")
        y_diff = _diff_attention(proj_a, diff_lambda[l], diff_head_norm[l], slopes, b, s, n_heads, lam_init)
        y_rwkv = _rwkv7(proj_r, rwkv_mu[l], rwkv_w0[l], rwkv_w_up[l], rwkv_a0[l], rwkv_a_up[l], rwkv_g_up[l],
                        rwkv_k_k[l], rwkv_k_a[l], rwkv_r_k[l], rwkv_ln_w[l], rwkv_ln_b[l], b, s)
        xt = _matmul([y_diff, y_rwkv], w_out, l, 0, d, F32, tm, _tile(d, (512, 256, 128)), res=xt, name="out_proj")
        outs = _hier_moe(xt, ffn_norm[l], router_group[l], router_group_bias[l], router_expert[l],
                         router_expert_bias[l], expert_w_gate, expert_w_up, expert_w_down, l,
                         final_norm if last else attn_norm[l + 1], F32 if last else BF16, not last)
        if not last:
            xt, h = outs
    return outs[0].reshape(b, s, d)
```
